```python
import math
import jax
import jax.numpy as jnp
from jax import lax
import numpy as np

D_MODEL = 1024
BATCH = 8
SEQ = 8192
DEPTH = 4
DEC_BATCH = 32
DEC_SEQ = 2048
PAST_LEN = 128

D_CONV = D_MODEL
CONV_WIDTH = 3
D_MLSTM = D_MODEL
N_MLSTM_HEADS = 4
HEAD_DIM = D_MLSTM // N_MLSTM_HEADS
CHUNK = 128
N_GATE_SLOTS = 4
D_IN = 4 * D_CONV + 5 * D_MLSTM + N_GATE_SLOTS * N_MLSTM_HEADS + 2 * D_MODEL
EPS = 1e-6
NEG_BIG = -1e30

kernel_name = "hybrid_conv_mlstm_bidir_encoder"


def rms_norm(x, g):
    xf = x.astype(jnp.float32)
    y = xf * lax.rsqrt(jnp.mean(xf * xf, axis=-1, keepdims=True) + EPS)
    return (y * g.astype(jnp.float32)).astype(x.dtype)


def split_combined(p):
    sizes = [D_CONV, D_CONV, D_CONV, D_CONV,
             D_MLSTM, D_MLSTM, D_MLSTM, D_MLSTM, D_MLSTM,
             N_GATE_SLOTS * N_MLSTM_HEADS,
             2 * D_MODEL]
    idx = []
    acc = 0
    for s in sizes[:-1]:
        acc += s
        idx.append(acc)
    return jnp.split(p, idx, axis=-1)


def centred_dwconv3(u, w, b):
    up = jnp.pad(u, ((0, 0), (1, 1), (0, 0)))
    return up[:, :-2] * w[0] + up[:, 1:-1] * w[1] + up[:, 2:] * w[2] + b


def mlstm_chunkwise(q, k, v, i_pre, f_pre):
    bsz, seq, nh, d = q.shape
    nc = seq // CHUNK

    def chunks4(a):
        return a.reshape(bsz, nc, CHUNK, nh, d).transpose(1, 0, 3, 2, 4)

    def chunks3(a):
        return a.reshape(bsz, nc, CHUNK, nh).transpose(1, 0, 3, 2)

    logf = jax.nn.log_sigmoid(f_pre)
    mask = jnp.tril(jnp.ones((CHUNK, CHUNK), dtype=bool))

    def body(carry, inp):
        C, n, m = carry
        qc, kc, vc, ic, lfc = inp
        b = jnp.cumsum(lfc, axis=-1)
        Dlog = b[..., :, None] - b[..., None, :] + ic[..., None, :]
        Dlog = jnp.where(mask, Dlog, NEG_BIG)
        inter = b + m[..., None]
        m_t = jnp.maximum(jnp.max(Dlog, axis=-1), inter)
        Dexp = jnp.exp(Dlog - m_t[..., None])
        inter_w = jnp.exp(inter - m_t)
        S = jnp.einsum('bhtd,bhsd->bhts', qc, kc) * Dexp
        num = jnp.einsum('bhts,bhsd->bhtd', S, vc) + inter_w[..., None] * jnp.einsum('bhtd,bhde->bhte', qc, C)
        den_raw = jnp.sum(S, axis=-1) + inter_w * jnp.einsum('bhtd,bhd->bht', qc, n)
        den = jnp.maximum(jnp.abs(den_raw), jnp.exp(-m_t))
        h = num / den[..., None]
        bL = b[..., -1]
        a = bL[..., None] - b + ic
        m_new = jnp.maximum(bL + m, jnp.max(a, axis=-1))
        w = jnp.exp(a - m_new[..., None])
        decay = jnp.exp(bL + m - m_new)
        C_new = decay[..., None, None] * C + jnp.einsum('bhs,bhsd,bhse->bhde', w, kc, vc)
        n_new = decay[..., None] * n + jnp.einsum('bhs,bhsd->bhd', w, kc)
        return (C_new, n_new, m_new), h

    init = (jnp.zeros((bsz, nh, d, d), jnp.float32),
            jnp.zeros((bsz, nh, d), jnp.float32),
            jnp.zeros((bsz, nh), jnp.float32))
    _, hs = lax.scan(body, init, (chunks4(q), chunks4(k), chunks4(v), chunks3(i_pre), chunks3(logf)))
    return hs.transpose(1, 0, 3, 2, 4).reshape(bsz, seq, nh, d)


def mixer_layer(x, c, w_ada, b_ada, norm_g, w_in, b_gates, conv_w, conv_b, mh_norm_g,
                w_proj_conv, w_proj_mlstm, w_out):
    bsz, seq, _ = x.shape
    mod = c @ w_ada + b_ada
    shift, scale, gate = jnp.split(mod[:, None, :], 3, axis=-1)
    h = rms_norm(x, norm_g) * (1.0 + scale) + shift
    p = h @ w_in
    cb, cc, cx, cz, q, k, v, o, mz, gpre, gmerge = split_combined(p)

    y_conv = cb * centred_dwconv3(cc * cx, conv_w, conv_b) * jax.nn.silu(cz)

    qf = q.astype(jnp.float32).reshape(bsz, seq, N_MLSTM_HEADS, HEAD_DIM)
    kf = k.astype(jnp.float32).reshape(bsz, seq, N_MLSTM_HEADS, HEAD_DIM) * (HEAD_DIM ** -0.5)
    vf = v.astype(jnp.float32).reshape(bsz, seq, N_MLSTM_HEADS, HEAD_DIM)
    g = gpre.astype(jnp.float32).reshape(bsz, seq, N_GATE_SLOTS, N_MLSTM_HEADS) + b_gates.astype(jnp.float32)
    h_fwd = mlstm_chunkwise(qf, kf, vf, g[:, :, 0], g[:, :, 1])
    h_bwd = jnp.flip(mlstm_chunkwise(jnp.flip(qf, 1), jnp.flip(kf, 1), jnp.flip(vf, 1),
                                     jnp.flip(g[:, :, 2], 1), jnp.flip(g[:, :, 3], 1)), 1)
    hm = h_fwd + h_bwd
    hm = hm * lax.rsqrt(jnp.mean(hm * hm, axis=-1, keepdims=True) + EPS)
    hm = hm * mh_norm_g.astype(jnp.float32).reshape(N_MLSTM_HEADS, HEAD_DIM)
    hm = hm.reshape(bsz, seq, D_MLSTM).astype(x.dtype)
    y_m = jax.nn.sigmoid(o) * hm * jax.nn.silu(mz)

    g_conv, g_mlstm = jnp.split(jax.nn.sigmoid(gmerge), 2, axis=-1)
    merged = g_conv * (y_conv @ w_proj_conv) + g_mlstm * (y_m @ w_proj_mlstm)
    return x + gate * (merged @ w_out)


def setup_inputs(seed: int = 0) -> dict:
    key = jax.random.key(seed)
    ks = jax.random.split(key, 20)
    f32 = jnp.float32
    D = D_MODEL
    nrm = lambda k, shape, s: jax.random.normal(k, shape, f32) * s
    forget_base = 3.0 + jnp.linspace(0.0, 3.0, N_MLSTM_HEADS, dtype=f32)
    b_gates = jnp.stack([
        nrm(ks[0], (DEPTH, N_MLSTM_HEADS), 0.1),
        forget_base + nrm(ks[1], (DEPTH, N_MLSTM_HEADS), 0.1),
        nrm(ks[2], (DEPTH, N_MLSTM_HEADS), 0.1),
        forget_base + nrm(ks[3], (DEPTH, N_MLSTM_HEADS), 0.1),
    ], axis=1)
    return {
        "x_prompt": nrm(ks[4], (BATCH, SEQ, D), 1.0),
        "x_sample": nrm(ks[5], (DEC_BATCH, DEC_SEQ, D), 1.0),
        "c_prompt": nrm(ks[6], (BATCH, D), 1.0),
        "c_sample": nrm(ks[7], (DEC_BATCH, D), 1.0),
        "w_ada": nrm(ks[8], (DEPTH, D, 3 * D), 0.1 * D ** -0.5),
        "b_ada": nrm(ks[9], (DEPTH, 3 * D), 0.02),
        "norm_g": 1.0 + nrm(ks[10], (DEPTH, D), 0.02),
        "w_in": nrm(ks[11], (DEPTH, D, D_IN), D ** -0.5),
        "b_gates": b_gates,
        "conv_w": nrm(ks[12], (DEPTH, CONV_WIDTH, D_CONV), CONV_WIDTH ** -0.5),
        "conv_b": nrm(ks[13], (DEPTH, D_CONV), 0.02),
        "mh_norm_g": 1.0 + nrm(ks[14], (DEPTH, D_MLSTM), 0.02),
        "w_proj_conv": nrm(ks[15], (DEPTH, D_CONV, D), D_CONV ** -0.5),
        "w_proj_mlstm": nrm(ks[16], (DEPTH, D_MLSTM, D), D_MLSTM ** -0.5),
        "w_out": nrm(ks[17], (DEPTH, D, D), D ** -0.5),
        "final_norm_g": 1.0 + nrm(ks[18], (D,), 0.02),
    }


def reference(x_prompt, x_sample, c_prompt, c_sample, w_ada, b_ada, norm_g, w_in, b_gates,
              conv_w, conv_b, mh_norm_g, w_proj_conv, w_proj_mlstm, w_out, final_norm_g):
    def trunk(x, c):
        for l in range(DEPTH):
            x = mixer_layer(x, c, w_ada[l], b_ada[l], norm_g[l], w_in[l], b_gates[l], conv_w[l],
                            conv_b[l], mh_norm_g[l], w_proj_conv[l], w_proj_mlstm[l], w_out[l])
        return rms_norm(x, final_norm_g)

    y_prompt = trunk(x_prompt, c_prompt)
    y_sample = trunk(x_sample, c_sample)
    return (y_prompt, y_sample)
```

```python
import functools

import jax
import jax.numpy as jnp
from jax import lax
from jax.experimental import pallas as pl
from jax.experimental.pallas import tpu as pltpu

F32 = jnp.float32
BF16 = jnp.bfloat16

D_MODEL = 1024
N_HEADS = 4
HEAD_DIM = D_MODEL // N_HEADS
CHUNK = 128
N_GATES = 16
EPS = 1e-6
NEG_BIG = -1e30

V7X_VMEM_BYTES = 64 * 2**20
V7X_LANES = 128
V7X_MXU_WIDTH = 256

OFF_CB, OFF_CC, OFF_CX, OFF_CZ = 0, 1 * D_MODEL, 2 * D_MODEL, 3 * D_MODEL
OFF_Q, OFF_V, OFF_O, OFF_MZ = 4 * D_MODEL, 5 * D_MODEL, 6 * D_MODEL, 7 * D_MODEL
OFF_GA, OFF_GB = 8 * D_MODEL, 9 * D_MODEL
W_A_COLS = 10 * D_MODEL

GATE_PERM = (0, 1, 2, 3, 8, 9, 10, 11, 4, 5, 6, 7, 12, 13, 14, 15)

PROJ_TILE = 512
COL_BLOCK = 256
MLSTM_CHUNKS = 4


def _sigmoid(x):
    return jax.nn.sigmoid(x)


def _rms_scale(x):
    return lax.rsqrt(jnp.mean(x * x, axis=-1, keepdims=True) + EPS)


def _ada_kernel(c_ref, w_ref, b_ref, g_ref, a_ref, shift_ref, gate_ref):
    mod = jnp.dot(c_ref[...], w_ref[0], preferred_element_type=F32,
                  precision=lax.Precision.HIGHEST) + b_ref[0]
    shift_ref[0] = mod[:, :D_MODEL]
    a_ref[0] = g_ref[0] * (1.0 + mod[:, D_MODEL:2 * D_MODEL])
    gate_ref[0] = mod[:, 2 * D_MODEL:]


def _ada(c_all, w_ada, b_ada, norm_g):
    depth = w_ada.shape[0]
    rows = c_all.shape[0]
    out = jax.ShapeDtypeStruct((depth, rows, D_MODEL), F32)
    vec = pl.BlockSpec((1, rows, D_MODEL), lambda l: (l, 0, 0))
    return pl.pallas_call(
        _ada_kernel,
        grid=(depth,),
        in_specs=[
            pl.BlockSpec((rows, D_MODEL), lambda l: (0, 0)),
            pl.BlockSpec((1, D_MODEL, 3 * D_MODEL), lambda l: (l, 0, 0)),
            pl.BlockSpec((1, 1, 3 * D_MODEL), lambda l: (l, 0, 0)),
            pl.BlockSpec((1, 1, D_MODEL), lambda l: (l, 0, 0)),
        ],
        out_specs=[vec, vec, vec],
        out_shape=[out, out, out],
        compiler_params=pltpu.CompilerParams(
            dimension_semantics=("arbitrary",), vmem_limit_bytes=40 * 2**20),
        name="ada",
    )(c_all, w_ada, b_ada.reshape(depth, 1, 3 * D_MODEL), norm_g.reshape(depth, 1, D_MODEL))


def _halo_kernel(x_ref, a_ref, s_ref, wcc_ref, wcx_ref, u_ref):
    x = x_ref[...]
    h = (x * _rms_scale(x) * a_ref[...] + s_ref[...]).astype(BF16)
    cc = jnp.dot(h, wcc_ref[...], preferred_element_type=F32)
    cx = jnp.dot(h, wcx_ref[...], preferred_element_type=F32)
    u_ref[...] = cc * cx


def _halo(xb, a_rows, s_rows, w_a):
    rows = xb.shape[0]
    full = pl.BlockSpec((rows, D_MODEL), lambda i: (0, 0))
    return pl.pallas_call(
        _halo_kernel,
        grid=(1,),
        in_specs=[full, full, full,
                  pl.BlockSpec((D_MODEL, D_MODEL), lambda i: (0, OFF_CC // D_MODEL)),
                  pl.BlockSpec((D_MODEL, D_MODEL), lambda i: (0, OFF_CX // D_MODEL))],
        out_specs=full,
        out_shape=jax.ShapeDtypeStruct((rows, D_MODEL), F32),
        name="halo",
    )(xb, a_rows, s_rows, w_a, w_a)


def _lane_scan(x, lane, op, fill, suffix):
    for s in (1, 2, 4, 8, 16, 32, 64):
        if suffix:
            shifted = jnp.where(lane < V7X_LANES - s, pltpu.roll(x, V7X_LANES - s, 1), fill)
        else:
            shifted = jnp.where(lane >= s, pltpu.roll(x, s, 1), fill)
        x = op(x, shifted)
    return x


def _proj_kernel(x_ref, uh_ref, a_ref, s_ref, wa_ref, wkg_ref, wpc_ref, cw_ref, cbias_ref, bg_ref,
                 q_ref, kt_ref, v_ref, og_ref, gmb_ref, yc_ref, rrow_ref, ccol_ref, cs_ref,
                 h_scr, yconv_scr):
    ts = x_ref.shape[1]
    n_chunks = ts // CHUNK
    x = x_ref[0]
    h_scr[...] = (x * _rms_scale(x) * a_ref[0] + s_ref[0]).astype(BF16)

    def mm(col, width=COL_BLOCK):
        return jnp.dot(h_scr[...], wa_ref[:, col:col + width], preferred_element_type=F32)

    row = lax.broadcasted_iota(jnp.int32, (ts, COL_BLOCK), 0)
    for c0 in range(0, D_MODEL, COL_BLOCK):
        cols = slice(c0, c0 + COL_BLOCK)
        u = mm(OFF_CC + c0) * mm(OFF_CX + c0)
        u_prev = jnp.where(row == 0, uh_ref[0, 0, 0:1, cols], pltpu.roll(u, 1, 0))
        u_next = jnp.where(row == ts - 1, uh_ref[0, 0, 1:2, cols], pltpu.roll(u, ts - 1, 0))
        conv = (u_prev * cw_ref[0:1, cols] + u * cw_ref[1:2, cols] + u_next * cw_ref[2:3, cols]
                + cbias_ref[0:1, cols])
        cz = mm(OFF_CZ + c0)
        y = mm(OFF_CB + c0) * conv * (cz * _sigmoid(cz))
        yconv_scr[:, cols] = y.astype(BF16)

    for c0 in range(0, D_MODEL, COL_BLOCK):
        cols = slice(c0, c0 + COL_BLOCK)
        yc = jnp.dot(yconv_scr[...], wpc_ref[:, cols], preferred_element_type=F32)
        yc_ref[0, :, cols] = (_sigmoid(mm(OFF_GA + c0)) * yc).astype(BF16)
        gmb_ref[0, :, cols] = _sigmoid(mm(OFF_GB + c0)).astype(BF16)
        mz = mm(OFF_MZ + c0)
        og_ref[0, :, cols] = (_sigmoid(mm(OFF_O + c0)) * (mz * _sigmoid(mz))).astype(BF16)
        q_ref[0, :, :, cols] = mm(OFF_Q + c0).astype(BF16).reshape(n_chunks, CHUNK, COL_BLOCK)
        v_ref[0, :, :, cols] = mm(OFF_V + c0).astype(BF16).reshape(n_chunks, CHUNK, COL_BLOCK)

    kg = lax.dot_general(wkg_ref[...], h_scr[...], (((1,), (1,)), ((), ())),
                         preferred_element_type=F32)
    for c in range(n_chunks):
        kt_ref[0, c] = kg[:D_MODEL, c * CHUNK:(c + 1) * CHUNK].astype(BF16)

    gates = kg[D_MODEL:, :] + bg_ref[...]
    gate_i = gates[:8]
    log_f = jax.nn.log_sigmoid(gates[8:])
    lane = lax.broadcasted_iota(jnp.int32, (8, CHUNK), 1)
    is_fwd = lax.broadcasted_iota(jnp.int32, (8, CHUNK), 0) < N_HEADS
    for c in range(n_chunks):
        lf = log_f[:, c * CHUNK:(c + 1) * CHUNK]
        b = jnp.where(is_fwd, _lane_scan(lf, lane, jnp.add, 0.0, False),
                      _lane_scan(lf, lane, jnp.add, 0.0, True))
        r = gate_i[:, c * CHUNK:(c + 1) * CHUNK] - b
        cmr = jnp.where(is_fwd, _lane_scan(r, lane, jnp.maximum, NEG_BIG, False),
                        _lane_scan(r, lane, jnp.maximum, NEG_BIG, True))
        rrow_ref[0, c] = r
        cs_ref[0, c, 0:8, :] = jnp.broadcast_to(jnp.max(r, axis=1, keepdims=True), (8, CHUNK))
        cs_ref[0, c, 8:16, :] = jnp.broadcast_to(jnp.sum(lf, axis=1, keepdims=True), (8, CHUNK))
        cols_t = jnp.concatenate([cmr, b, jnp.zeros((CHUNK - 16, CHUNK), F32)], axis=0).T
        ccol_ref[0, c] = cols_t[:, :16]


def _proj(x, uh, a, s, w_a, w_kg, w_pc, conv_w, conv_b, b_g):
    bsz, seq, _ = x.shape
    ts = PROJ_TILE
    n_tiles = seq // ts
    n_chunks = seq // CHUNK
    cps = ts // CHUNK

    def const(shape):
        return pl.BlockSpec(shape, lambda b, i: (0,) * len(shape), pipeline_mode=pl.Buffered(1))

    tok = pl.BlockSpec((1, ts, D_MODEL), lambda b, i: (b, i, 0))
    per_row = pl.BlockSpec((1, 1, D_MODEL), lambda b, i: (b, 0, 0))
    chunked = lambda r, c: pl.BlockSpec((1, cps, r, c), lambda b, i: (b, i, 0, 0))
    tok_bf16 = jax.ShapeDtypeStruct((bsz, seq, D_MODEL), BF16)
    return pl.pallas_call(
        _proj_kernel,
        grid=(bsz, n_tiles),
        in_specs=[
            tok,
            pl.BlockSpec((1, 1, 8, D_MODEL), lambda b, i: (b, i, 0, 0)),
            per_row, per_row,
            const((D_MODEL, W_A_COLS)), const((D_MODEL + N_GATES, D_MODEL)),
            const((D_MODEL, D_MODEL)), const((3, D_MODEL)), const((1, D_MODEL)),
            const((N_GATES, 1)),
        ],
        out_specs=[
            chunked(CHUNK, D_MODEL), chunked(D_MODEL, CHUNK), chunked(CHUNK, D_MODEL),
            tok, tok, tok,
            chunked(8, CHUNK), chunked(CHUNK, 16), chunked(16, CHUNK),
        ],
        out_shape=[
            jax.ShapeDtypeStruct((bsz, n_chunks, CHUNK, D_MODEL), BF16),
            jax.ShapeDtypeStruct((bsz, n_chunks, D_MODEL, CHUNK), BF16),
            jax.ShapeDtypeStruct((bsz, n_chunks, CHUNK, D_MODEL), BF16),
            tok_bf16, tok_bf16, tok_bf16,
            jax.ShapeDtypeStruct((bsz, n_chunks, 8, CHUNK), F32),
            jax.ShapeDtypeStruct((bsz, n_chunks, CHUNK, 16), F32),
            jax.ShapeDtypeStruct((bsz, n_chunks, 16, CHUNK), F32),
        ],
        scratch_shapes=[pltpu.VMEM((ts, D_MODEL), BF16), pltpu.VMEM((ts, D_MODEL), BF16)],
        compiler_params=pltpu.CompilerParams(
            dimension_semantics=("parallel", "parallel"), vmem_limit_bytes=56 * 2**20),
        name="proj",
    )(x, uh, a, s, w_a, w_kg, w_pc, conv_w, conv_b, b_g)


def _mlstm_kernel(qf_ref, ktf_ref, vf_ref, rf_ref, cf_ref, sf_ref,
                  qb_ref, ktb_ref, vb_ref, rb_ref, cb_ref, sb_ref,
                  hf_ref, hb_ref,
                  c_scr, cbf_scr, n_scr, nbf_scr, m_scr):
    cps = qf_ref.shape[1]

    @pl.when(pl.program_id(1) == 0)
    def _():
        c_scr[...] = jnp.zeros_like(c_scr)
        cbf_scr[...] = jnp.zeros_like(cbf_scr)
        n_scr[...] = jnp.zeros_like(n_scr)
        nbf_scr[...] = jnp.zeros_like(nbf_scr)
        m_scr[...] = jnp.zeros_like(m_scr)

    t_idx = lax.broadcasted_iota(jnp.int32, (CHUNK, CHUNK), 0)
    s_idx = lax.broadcasted_iota(jnp.int32, (CHUNK, CHUNK), 1)
    masks = (s_idx <= t_idx, s_idx >= t_idx)
    dirs = ((qf_ref, ktf_ref, vf_ref, rf_ref, cf_ref, sf_ref, hf_ref),
            (qb_ref, ktb_ref, vb_ref, rb_ref, cb_ref, sb_ref, hb_ref))

    def chunk_body(step, carry):
        for d, (q_ref, kt_ref, v_ref, r_ref, col_ref, cs_ref, h_ref) in enumerate(dirs):
            c = step if d == 0 else cps - 1 - step
            for hh in range(N_HEADS):
                i = d * N_HEADS + hh
                hs = slice(hh * HEAD_DIM, (hh + 1) * HEAD_DIM)
                q = q_ref[0, c, :, hs]
                kt = kt_ref[0, c, hs, :]
                v = v_ref[0, c, :, hs]
                r_row = r_ref[0, c, i:i + 1, :]
                cmr_col = col_ref[0, c, :, i:i + 1]
                b_col = col_ref[0, c, :, 8 + i:9 + i]
                r_max = cs_ref[0, c, i:i + 1, :]
                b_tot = cs_ref[0, c, 8 + i:9 + i, :]
                m_row = m_scr[i]
                m_one = m_row[:, 0:1]

                big_m = jnp.maximum(cmr_col, m_one)
                d_exp = jnp.exp(jnp.where(masks[d], r_row - big_m, NEG_BIG))
                s_mat = jnp.dot(q, kt, preferred_element_type=F32) * d_exp
                inter_w = jnp.exp(m_one - big_m)
                q_c = jnp.dot(q, cbf_scr[i], preferred_element_type=F32)
                q_n = jnp.dot(q, nbf_scr[i], preferred_element_type=F32)[:, 0:1]
                num = jnp.dot(s_mat.astype(BF16), v, preferred_element_type=F32) + inter_w * q_c
                den_raw = jnp.sum(s_mat, axis=1, keepdims=True) + inter_w * q_n
                den = jnp.maximum(jnp.abs(den_raw), jnp.exp(-(b_col + big_m)))
                h_ref[0, c, :, hs] = (num * (1.0 / den)).astype(BF16)

                m_x = jnp.maximum(m_row, r_max)
                w_row = jnp.exp(r_row - m_x)
                decay = jnp.exp(m_row - m_x)
                ktw = kt * w_row.astype(BF16)
                c_new = decay[:, 0:1] * c_scr[i] + jnp.dot(ktw, v, preferred_element_type=F32)
                c_scr[i] = c_new
                cbf_scr[i] = c_new.astype(BF16)
                n_new = decay * n_scr[i] + jnp.sum(ktw.astype(F32), axis=1, keepdims=True)
                n_scr[i] = n_new
                nbf_scr[i] = n_new.astype(BF16)
                m_scr[i] = b_tot + m_x
        return carry

    lax.fori_loop(0, cps, chunk_body, 0)


def _mlstm(q4, kt4, v4, rrow, ccol, cs):
    bsz, n_chunks = q4.shape[:2]
    cps = MLSTM_CHUNKS
    n_steps = n_chunks // cps

    def specs(index):
        blk = lambda r, c: pl.BlockSpec((1, cps, r, c), lambda b, j: (b, index(j), 0, 0))
        return [blk(CHUNK, D_MODEL), blk(D_MODEL, CHUNK), blk(CHUNK, D_MODEL),
                blk(8, CHUNK), blk(CHUNK, 16), blk(16, CHUNK)]

    fwd = lambda j: j
    bwd = lambda j: n_steps - 1 - j
    h_shape = jax.ShapeDtypeStruct(q4.shape, BF16)
    n_state = 2 * N_HEADS
    return pl.pallas_call(
        _mlstm_kernel,
        grid=(bsz, n_steps),
        in_specs=specs(fwd) + specs(bwd),
        out_specs=[specs(fwd)[0], specs(bwd)[0]],
        out_shape=[h_shape, h_shape],
        scratch_shapes=[
            pltpu.VMEM((n_state, HEAD_DIM, HEAD_DIM), F32),
            pltpu.VMEM((n_state, HEAD_DIM, HEAD_DIM), BF16),
            pltpu.VMEM((n_state, HEAD_DIM, CHUNK), F32),
            pltpu.VMEM((n_state, HEAD_DIM, CHUNK), BF16),
            pltpu.VMEM((n_state, 1, CHUNK), F32),
        ],
        compiler_params=pltpu.CompilerParams(
            dimension_semantics=("parallel", "arbitrary"), vmem_limit_bytes=48 * 2**20),
        name="mlstm",
    )(q4, kt4, v4, rrow, ccol, cs, q4, kt4, v4, rrow, ccol, cs)


def _out_kernel(hf_ref, hb_ref, og_ref, gmb_ref, yc_ref, x_ref, gate_ref, mhg_ref, wpm_ref,
                wout_ref, fg_ref, o_ref, ym_scr, *, final):
    for hh in range(N_HEADS):
        hs = slice(hh * HEAD_DIM, (hh + 1) * HEAD_DIM)
        hm = hf_ref[0, :, hs].astype(F32) + hb_ref[0, :, hs].astype(F32)
        hm = hm * _rms_scale(hm) * mhg_ref[0:1, hs]
        ym_scr[:, hs] = (og_ref[0, :, hs].astype(F32) * hm).astype(BF16)
    ym = jnp.dot(ym_scr[...], wpm_ref[...], preferred_element_type=F32)
    merged = yc_ref[0].astype(F32) + gmb_ref[0].astype(F32) * ym
    y = jnp.dot(merged.astype(BF16), wout_ref[...], preferred_element_type=F32)
    x_new = x_ref[0] + gate_ref[0] * y
    if final:
        x_new = x_new * _rms_scale(x_new) * fg_ref[...]
    o_ref[0] = x_new


def _out(hf, hb, og, gmb, yc, x, gate, mh_g, w_pm, w_out, final_g, final):
    bsz, seq, _ = x.shape
    ts = PROJ_TILE
    tok = pl.BlockSpec((1, ts, D_MODEL), lambda b, i: (b, i, 0))
    const = lambda shape: pl.BlockSpec(shape, lambda b, i: (0,) * len(shape))
    return pl.pallas_call(
        functools.partial(_out_kernel, final=final),
        grid=(bsz, seq // ts),
        in_specs=[tok, tok, tok, tok, tok, tok,
                  pl.BlockSpec((1, 1, D_MODEL), lambda b, i: (b, 0, 0)),
                  const((1, D_MODEL)), const((D_MODEL, D_MODEL)), const((D_MODEL, D_MODEL)),
                  const((1, D_MODEL))],
        out_specs=tok,
        out_shape=jax.ShapeDtypeStruct(x.shape, F32),
        scratch_shapes=[pltpu.VMEM((ts, D_MODEL), BF16)],
        compiler_params=pltpu.CompilerParams(
            dimension_semantics=("parallel", "parallel"), vmem_limit_bytes=48 * 2**20),
        name="out",
    )(hf, hb, og, gmb, yc, x, gate, mh_g, w_pm, w_out, final_g)


def _halo_rows(x):
    return jnp.concatenate([x[:, PROJ_TILE - 1::PROJ_TILE], x[:, ::PROJ_TILE]], axis=1)


def _halo_blocks(u, bsz, n_tiles):
    zero = jnp.zeros((bsz, 1, D_MODEL), F32)
    prev = jnp.concatenate([zero, u[:, :n_tiles - 1]], axis=1)
    nxt = jnp.concatenate([u[:, n_tiles + 1:], zero], axis=1)
    blk = jnp.stack([prev, nxt], axis=2)
    return jnp.pad(blk, ((0, 0), (0, 0), (0, 6), (0, 0)))


def kernel(x_prompt, x_sample, c_prompt, c_sample, w_ada, b_ada, norm_g, w_in, b_gates, conv_w,
           conv_b, mh_norm_g, w_proj_conv, w_proj_mlstm, w_out, final_norm_g):
    depth = w_ada.shape[0]
    xs = [x_prompt, x_sample]
    n_rows = [x.shape[0] for x in xs]
    a_all, shift_all, gate_all = _ada(jnp.concatenate([c_prompt, c_sample], axis=0),
                                      w_ada, b_ada, norm_g)
    perm = jnp.array(GATE_PERM)
    final_g = final_norm_g.reshape(1, D_MODEL)

    for l in range(depth):
        w = w_in[l]
        blocks = [w[:, i * D_MODEL:(i + 1) * D_MODEL] for i in range(9)]
        cb_, cc_, cx_, cz_, q_, k_, v_, o_, mz_ = blocks
        w_g = w[:, 9 * D_MODEL:9 * D_MODEL + N_GATES][:, perm]
        w_gm = w[:, 9 * D_MODEL + N_GATES:]
        w_a = jnp.concatenate([cb_, cc_, cx_, cz_, q_, v_, o_, mz_, w_gm], axis=1).astype(BF16)
        w_kg = jnp.concatenate([k_ * (HEAD_DIM ** -0.5), w_g], axis=1).T.astype(BF16)
        w_pc = w_proj_conv[l].astype(BF16)
        w_pm = w_proj_mlstm[l].astype(BF16)
        w_o = w_out[l].astype(BF16)
        b_g = b_gates[l].reshape(N_GATES)[perm].reshape(N_GATES, 1)
        cbias = conv_b[l].reshape(1, D_MODEL)
        mh_g = mh_norm_g[l].reshape(1, D_MODEL)

        row0 = 0
        xb, a_rows, s_rows = [], [], []
        for x, nb in zip(xs, n_rows):
            hb = _halo_rows(x)
            reps = hb.shape[1]
            xb.append(hb.reshape(-1, D_MODEL))
            a_rows.append(jnp.repeat(a_all[l, row0:row0 + nb], reps, axis=0))
            s_rows.append(jnp.repeat(shift_all[l, row0:row0 + nb], reps, axis=0))
            row0 += nb
        u_halo = _halo(jnp.concatenate(xb), jnp.concatenate(a_rows), jnp.concatenate(s_rows), w_a)

        row0, urow0 = 0, 0
        new_xs = []
        for x, nb in zip(xs, n_rows):
            bsz, seq, _ = x.shape
            n_tiles = seq // PROJ_TILE
            n_u = bsz * 2 * n_tiles
            uh = _halo_blocks(u_halo[urow0:urow0 + n_u].reshape(bsz, 2 * n_tiles, D_MODEL),
                              bsz, n_tiles)
            a = a_all[l, row0:row0 + nb].reshape(nb, 1, D_MODEL)
            s = shift_all[l, row0:row0 + nb].reshape(nb, 1, D_MODEL)
            gate = gate_all[l, row0:row0 + nb].reshape(nb, 1, D_MODEL)
            q4, kt4, v4, og, gmb, yc, rrow, ccol, cs = _proj(
                x, uh, a, s, w_a, w_kg, w_pc, conv_w[l], cbias, b_g)
            hf, hb = _mlstm(q4, kt4, v4, rrow, ccol, cs)
            new_xs.append(_out(hf.reshape(x.shape), hb.reshape(x.shape), og, gmb, yc, x, gate,
                               mh_g, w_pm, w_o, final_g, final=(l == depth - 1)))
            row0 += nb
            urow0 += n_u
        xs = new_xs
    return tuple(xs)
```

```python
import functools

import jax
import jax.numpy as jnp
from jax import lax
from jax.experimental import pallas as pl
from jax.experimental.pallas import tpu as pltpu

F32 = jnp.float32
BF16 = jnp.bfloat16

D_MODEL = 1024
N_HEADS = 4
HEAD_DIM = D_MODEL // N_HEADS
N_GATES = 16
N_STATES = 2 * N_HEADS
EPS = 1e-6
NEG_BIG = -1e30

V7X_LANES = 128
V7X_SUBLANES = 8
V7X_MXU_WIDTH = 256

OFF_CB, OFF_CC, OFF_CX, OFF_CZ = 0, 1 * D_MODEL, 2 * D_MODEL, 3 * D_MODEL
OFF_Q, OFF_V, OFF_O, OFF_MZ = 4 * D_MODEL, 5 * D_MODEL, 6 * D_MODEL, 7 * D_MODEL
OFF_GA, OFF_GB = 8 * D_MODEL, 9 * D_MODEL
W_A_COLS = 10 * D_MODEL

GATE_PERM = (0, 1, 2, 3, 8, 9, 10, 11, 4, 5, 6, 7, 12, 13, 14, 15)

PROJ_TILE = 512
COL_BLOCK = 256
MCHUNK = 256
MLSTM_CHUNKS = 4
EDGE_ROWS = V7X_SUBLANES


def _sigmoid(x):
    return jax.nn.sigmoid(x)


def _rms_scale(x):
    return lax.rsqrt(jnp.mean(x * x, axis=-1, keepdims=True) + EPS)


def _twice(x):
    return jnp.concatenate([x, x], axis=1)


def _ada_kernel(c_ref, w_ref, b_ref, g_ref, a_ref, shift_ref, gate_ref):
    mod = jnp.dot(c_ref[...], w_ref[0], preferred_element_type=F32,
                  precision=lax.Precision.HIGHEST) + b_ref[0]
    shift_ref[0] = mod[:, :D_MODEL]
    a_ref[0] = g_ref[0] * (1.0 + mod[:, D_MODEL:2 * D_MODEL])
    gate_ref[0] = mod[:, 2 * D_MODEL:]


def _ada(c_all, w_ada, b_ada, norm_g):
    depth = w_ada.shape[0]
    rows = c_all.shape[0]
    out = jax.ShapeDtypeStruct((depth, rows, D_MODEL), F32)
    vec = pl.BlockSpec((1, rows, D_MODEL), lambda l: (l, 0, 0))
    return pl.pallas_call(
        _ada_kernel,
        grid=(depth,),
        in_specs=[
            pl.BlockSpec((rows, D_MODEL), lambda l: (0, 0)),
            pl.BlockSpec((1, D_MODEL, 3 * D_MODEL), lambda l: (l, 0, 0)),
            pl.BlockSpec((1, 1, 3 * D_MODEL), lambda l: (l, 0, 0)),
            pl.BlockSpec((1, 1, D_MODEL), lambda l: (l, 0, 0)),
        ],
        out_specs=[vec, vec, vec],
        out_shape=[out, out, out],
        compiler_params=pltpu.CompilerParams(
            dimension_semantics=("arbitrary",), vmem_limit_bytes=40 * 2**20),
        name="ada",
    )(c_all, w_ada, b_ada.reshape(depth, 1, 3 * D_MODEL), norm_g.reshape(depth, 1, D_MODEL))


def _write_edges(e_ref, first_row, last_row):
    e_ref[0, 0, 0:1, :] = first_row
    e_ref[0, 0, 1:2, :] = last_row
    e_ref[0, 0, 2:EDGE_ROWS, :] = jnp.zeros((EDGE_ROWS - 2, D_MODEL), F32)


def _edges_kernel(first_ref, last_ref, e_ref):
    _write_edges(e_ref, first_ref[0, 0:1, :], last_ref[0, V7X_SUBLANES - 1:V7X_SUBLANES, :])


def _edges(x):
    bsz, seq, _ = x.shape
    n_tiles = seq // PROJ_TILE
    blocks_per_tile = PROJ_TILE // V7X_SUBLANES
    return pl.pallas_call(
        _edges_kernel,
        grid=(bsz, n_tiles),
        in_specs=[
            pl.BlockSpec((1, V7X_SUBLANES, D_MODEL), lambda b, i: (b, i * blocks_per_tile, 0)),
            pl.BlockSpec((1, V7X_SUBLANES, D_MODEL),
                         lambda b, i: (b, (i + 1) * blocks_per_tile - 1, 0)),
        ],
        out_specs=pl.BlockSpec((1, 1, EDGE_ROWS, D_MODEL), lambda b, i: (b, i, 0, 0)),
        out_shape=jax.ShapeDtypeStruct((bsz, n_tiles, EDGE_ROWS, D_MODEL), F32),
        compiler_params=pltpu.CompilerParams(dimension_semantics=("parallel", "parallel")),
        name="edges",
    )(x, x)


def _halo_kernel(x_ref, a_ref, s_ref, wcc_ref, wcx_ref, u_ref):
    x = x_ref[0]
    h = (x * _rms_scale(x) * a_ref[0] + s_ref[0]).astype(BF16)
    cc = jnp.dot(h, wcc_ref[...], preferred_element_type=F32)
    cx = jnp.dot(h, wcx_ref[...], preferred_element_type=F32)
    u_ref[0] = cc * cx


def _halo(edges, a, s, w_a):
    bsz, n_tiles = edges.shape[:2]
    rows = n_tiles * EDGE_ROWS
    blk = pl.BlockSpec((1, rows, D_MODEL), lambda b: (b, 0, 0))
    per_row = pl.BlockSpec((1, 1, D_MODEL), lambda b: (b, 0, 0))
    u = pl.pallas_call(
        _halo_kernel,
        grid=(bsz,),
        in_specs=[blk, per_row, per_row,
                  pl.BlockSpec((D_MODEL, D_MODEL), lambda b: (0, OFF_CC // D_MODEL)),
                  pl.BlockSpec((D_MODEL, D_MODEL), lambda b: (0, OFF_CX // D_MODEL))],
        out_specs=blk,
        out_shape=jax.ShapeDtypeStruct((bsz, rows, D_MODEL), F32),
        compiler_params=pltpu.CompilerParams(dimension_semantics=("parallel",)),
        name="halo",
    )(edges.reshape(bsz, rows, D_MODEL), a, s, w_a, w_a)
    return u.reshape(edges.shape)


def _lane_scan(x, lane, op, fill, suffix):
    width = x.shape[1]
    s = 1
    while s < width:
        if suffix:
            shifted = jnp.where(lane < width - s, pltpu.roll(x, width - s, 1), fill)
        else:
            shifted = jnp.where(lane >= s, pltpu.roll(x, s, 1), fill)
        x = op(x, shifted)
        s *= 2
    return x


def _proj_kernel(x_ref, up_ref, un_ref, a_ref, s_ref, wa_ref, wkg_ref, wpc_ref, cw_ref, cbias_ref,
                 bg_ref, q_ref, kt_ref, v_ref, og_ref, gmb_ref, yc_ref, rrow_ref, ccol_ref, cs_ref,
                 h_scr, yconv_scr):
    ts = x_ref.shape[1]
    n_chunks = ts // MCHUNK
    tile = pl.program_id(1)
    x = x_ref[0]
    h_scr[...] = (x * _rms_scale(x) * a_ref[0] + s_ref[0]).astype(BF16)

    kg = lax.dot_general(wkg_ref[...], h_scr[...], (((1,), (1,)), ((), ())),
                         preferred_element_type=F32)
    for c in range(n_chunks):
        kt_ref[0, c] = kg[:D_MODEL, c * MCHUNK:(c + 1) * MCHUNK].astype(BF16)

    gates = kg[D_MODEL:, :] + bg_ref[...]
    gate_i = gates[:N_STATES]
    log_f = jax.nn.log_sigmoid(gates[N_STATES:])
    lane = lax.broadcasted_iota(jnp.int32, (N_STATES, MCHUNK), 1)
    is_fwd = lax.broadcasted_iota(jnp.int32, (N_STATES, MCHUNK), 0) < N_HEADS
    pad_rows = jnp.zeros((V7X_LANES - 2 * N_STATES, V7X_LANES), F32)
    for c in range(n_chunks):
        lf = log_f[:, c * MCHUNK:(c + 1) * MCHUNK]
        b = jnp.where(is_fwd, _lane_scan(lf, lane, jnp.add, 0.0, False),
                      _lane_scan(lf, lane, jnp.add, 0.0, True))
        r = gate_i[:, c * MCHUNK:(c + 1) * MCHUNK] - b
        cmr = jnp.where(is_fwd, _lane_scan(r, lane, jnp.maximum, NEG_BIG, False),
                        _lane_scan(r, lane, jnp.maximum, NEG_BIG, True))
        rrow_ref[0, c] = r
        cs_ref[0, c, 0:N_STATES, :] = jnp.broadcast_to(
            jnp.max(r, axis=1, keepdims=True), (N_STATES, V7X_LANES))
        cs_ref[0, c, N_STATES:, :] = jnp.broadcast_to(
            jnp.sum(lf, axis=1, keepdims=True), (N_STATES, V7X_LANES))
        for half in range(MCHUNK // V7X_LANES):
            ls = slice(half * V7X_LANES, (half + 1) * V7X_LANES)
            cols_t = jnp.concatenate([cmr[:, ls], b[:, ls], pad_rows], axis=0).T
            ccol_ref[0, c, ls, :] = cols_t[:, :2 * N_STATES]

    def mm(col, width=COL_BLOCK):
        return jnp.dot(h_scr[...], wa_ref[:, col:col + width], preferred_element_type=F32)

    row = lax.broadcasted_iota(jnp.int32, (ts, COL_BLOCK), 0)
    has_prev = tile > 0
    has_next = tile < pl.num_programs(1) - 1
    for c0 in range(0, D_MODEL, COL_BLOCK):
        cols = slice(c0, c0 + COL_BLOCK)
        u = mm(OFF_CC + c0) * mm(OFF_CX + c0)
        edge_prev = jnp.where(has_prev, up_ref[0, 0, 1:2, cols], 0.0)
        edge_next = jnp.where(has_next, un_ref[0, 0, 0:1, cols], 0.0)
        u_prev = jnp.where(row == 0, edge_prev, pltpu.roll(u, 1, 0))
        u_next = jnp.where(row == ts - 1, edge_next, pltpu.roll(u, ts - 1, 0))
        conv = (u_prev * cw_ref[0:1, cols] + u * cw_ref[1:2, cols] + u_next * cw_ref[2:3, cols]
                + cbias_ref[0:1, cols])
        cz = mm(OFF_CZ + c0)
        y = mm(OFF_CB + c0) * conv * (cz * _sigmoid(cz))
        yconv_scr[:, cols] = y.astype(BF16)

    for c0 in range(0, D_MODEL, COL_BLOCK):
        cols = slice(c0, c0 + COL_BLOCK)
        yc = jnp.dot(yconv_scr[...], wpc_ref[:, cols], preferred_element_type=F32)
        yc_ref[0, :, cols] = (_sigmoid(mm(OFF_GA + c0)) * yc).astype(BF16)
        gmb_ref[0, :, cols] = _sigmoid(mm(OFF_GB + c0)).astype(BF16)
        mz = mm(OFF_MZ + c0)
        og_ref[0, :, cols] = (_sigmoid(mm(OFF_O + c0)) * (mz * _sigmoid(mz))).astype(BF16)
        q_ref[0, :, :, cols] = mm(OFF_Q + c0).astype(BF16).reshape(n_chunks, MCHUNK, COL_BLOCK)
        v_ref[0, :, :, cols] = mm(OFF_V + c0).astype(BF16).reshape(n_chunks, MCHUNK, COL_BLOCK)


def _proj(x, u_edges, a, s, w_a, w_kg, w_pc, conv_w, conv_b, b_g):
    bsz, seq, _ = x.shape
    ts = PROJ_TILE
    n_tiles = seq // ts
    n_chunks = seq // MCHUNK
    cps = ts // MCHUNK

    def const(shape):
        return pl.BlockSpec(shape, lambda b, i: (0,) * len(shape), pipeline_mode=pl.Buffered(1))

    tok = pl.BlockSpec((1, ts, D_MODEL), lambda b, i: (b, i, 0))
    per_row = pl.BlockSpec((1, 1, D_MODEL), lambda b, i: (b, 0, 0))
    edge = lambda index: pl.BlockSpec((1, 1, EDGE_ROWS, D_MODEL),
                                      lambda b, i: (b, index(i), 0, 0))
    chunked = lambda r, c: pl.BlockSpec((1, cps, r, c), lambda b, i: (b, i, 0, 0))
    tok_bf16 = jax.ShapeDtypeStruct((bsz, seq, D_MODEL), BF16)
    return pl.pallas_call(
        _proj_kernel,
        grid=(bsz, n_tiles),
        in_specs=[
            tok,
            edge(lambda i: jnp.maximum(i - 1, 0)), edge(lambda i: jnp.minimum(i + 1, n_tiles - 1)),
            per_row, per_row,
            const((D_MODEL, W_A_COLS)), const((D_MODEL + N_GATES, D_MODEL)),
            const((D_MODEL, D_MODEL)), const((3, D_MODEL)), const((1, D_MODEL)),
            const((N_GATES, 1)),
        ],
        out_specs=[
            chunked(MCHUNK, D_MODEL), chunked(D_MODEL, MCHUNK), chunked(MCHUNK, D_MODEL),
            tok, tok, tok,
            chunked(N_STATES, MCHUNK), chunked(MCHUNK, 2 * N_STATES),
            chunked(2 * N_STATES, V7X_LANES),
        ],
        out_shape=[
            jax.ShapeDtypeStruct((bsz, n_chunks, MCHUNK, D_MODEL), BF16),
            jax.ShapeDtypeStruct((bsz, n_chunks, D_MODEL, MCHUNK), BF16),
            jax.ShapeDtypeStruct((bsz, n_chunks, MCHUNK, D_MODEL), BF16),
            tok_bf16, tok_bf16, tok_bf16,
            jax.ShapeDtypeStruct((bsz, n_chunks, N_STATES, MCHUNK), F32),
            jax.ShapeDtypeStruct((bsz, n_chunks, MCHUNK, 2 * N_STATES), F32),
            jax.ShapeDtypeStruct((bsz, n_chunks, 2 * N_STATES, V7X_LANES), F32),
        ],
        scratch_shapes=[pltpu.VMEM((ts, D_MODEL), BF16), pltpu.VMEM((ts, D_MODEL), BF16)],
        compiler_params=pltpu.CompilerParams(
            dimension_semantics=("parallel", "parallel"), vmem_limit_bytes=56 * 2**20),
        name="proj",
    )(x, u_edges, u_edges, a, s, w_a, w_kg, w_pc, conv_w, conv_b, b_g)


def _mlstm_kernel(qf_ref, ktf_ref, vf_ref, rf_ref, cf_ref, sf_ref,
                  qb_ref, ktb_ref, vb_ref, rb_ref, cb_ref, sb_ref,
                  hf_ref, hb_ref,
                  c_scr, cbf_scr, n_scr, nbf_scr, m_scr):
    cps = qf_ref.shape[1]

    @pl.when(pl.program_id(1) == 0)
    def _():
        c_scr[...] = jnp.zeros_like(c_scr)
        cbf_scr[...] = jnp.zeros_like(cbf_scr)
        n_scr[...] = jnp.zeros_like(n_scr)
        nbf_scr[...] = jnp.zeros_like(nbf_scr)
        m_scr[...] = jnp.zeros_like(m_scr)

    t_idx = lax.broadcasted_iota(jnp.int32, (MCHUNK, MCHUNK), 0)
    s_idx = lax.broadcasted_iota(jnp.int32, (MCHUNK, MCHUNK), 1)
    masks = (s_idx <= t_idx, s_idx >= t_idx)
    ones = jnp.ones((MCHUNK, V7X_LANES), BF16)
    dirs = ((qf_ref, ktf_ref, vf_ref, rf_ref, cf_ref, sf_ref, hf_ref),
            (qb_ref, ktb_ref, vb_ref, rb_ref, cb_ref, sb_ref, hb_ref))

    def chunk_body(step, carry):
        for d, (q_ref, kt_ref, v_ref, r_ref, col_ref, cs_ref, h_ref) in enumerate(dirs):
            c = step if d == 0 else cps - 1 - step
            for hh in range(N_HEADS):
                i = d * N_HEADS + hh
                hs = slice(hh * HEAD_DIM, (hh + 1) * HEAD_DIM)
                q = q_ref[0, c, :, hs]
                kt = kt_ref[0, c, hs, :]
                v = v_ref[0, c, :, hs]
                r_row = r_ref[0, c, i:i + 1, :]
                cmr = jnp.broadcast_to(col_ref[0, c, :, i:i + 1], (MCHUNK, V7X_LANES))
                b_cum = jnp.broadcast_to(col_ref[0, c, :, N_STATES + i:N_STATES + i + 1],
                                         (MCHUNK, V7X_LANES))
                r_max = cs_ref[0, c, i:i + 1, :]
                b_tot = cs_ref[0, c, N_STATES + i:N_STATES + i + 1, :]
                m_row = m_scr[i]

                big_m = jnp.maximum(cmr, m_row)
                d_exp = jnp.exp(jnp.where(masks[d], r_row - _twice(big_m), NEG_BIG))
                p = jnp.dot(q, kt, preferred_element_type=F32) * d_exp
                p_bf = p.astype(BF16)
                inter_w = jnp.exp(m_row - big_m)
                q_n = jnp.dot(q, nbf_scr[i], preferred_element_type=F32)
                p_sum = jnp.dot(p_bf, ones, preferred_element_type=F32)
                den = jnp.maximum(jnp.abs(p_sum + inter_w * q_n), jnp.exp(-(b_cum + big_m)))
                inv = 1.0 / den
                q_c = jnp.dot(q, cbf_scr[i], preferred_element_type=F32)
                num = jnp.dot(p_bf, v, preferred_element_type=F32) + _twice(inter_w) * q_c
                h_ref[0, c, :, hs] = (num * _twice(inv)).astype(BF16)

                m_x = jnp.maximum(m_row, r_max)
                w_row = jnp.exp(r_row - _twice(m_x))
                decay = jnp.exp(m_row - m_x)
                ktw = kt * w_row.astype(BF16)
                c_new = _twice(decay) * c_scr[i] + jnp.dot(ktw, v, preferred_element_type=F32)
                c_scr[i] = c_new
                cbf_scr[i] = c_new.astype(BF16)
                n_new = decay * n_scr[i] + jnp.dot(ktw, ones, preferred_element_type=F32)
                n_scr[i] = n_new
                nbf_scr[i] = n_new.astype(BF16)
                m_scr[i] = b_tot + m_x
        return carry

    lax.fori_loop(0, cps, chunk_body, 0)


def _mlstm(q4, kt4, v4, rrow, ccol, cs):
    bsz, n_chunks = q4.shape[:2]
    cps = min(MLSTM_CHUNKS, n_chunks)
    n_steps = n_chunks // cps

    def specs(index):
        blk = lambda r, c: pl.BlockSpec((1, cps, r, c), lambda b, j: (b, index(j), 0, 0))
        return [blk(MCHUNK, D_MODEL), blk(D_MODEL, MCHUNK), blk(MCHUNK, D_MODEL),
                blk(N_STATES, MCHUNK), blk(MCHUNK, 2 * N_STATES), blk(2 * N_STATES, V7X_LANES)]

    fwd = lambda j: j
    bwd = lambda j: n_steps - 1 - j
    h_shape = jax.ShapeDtypeStruct(q4.shape, BF16)
    return pl.pallas_call(
        _mlstm_kernel,
        grid=(bsz, n_steps),
        in_specs=specs(fwd) + specs(bwd),
        out_specs=[specs(fwd)[0], specs(bwd)[0]],
        out_shape=[h_shape, h_shape],
        scratch_shapes=[
            pltpu.VMEM((N_STATES, HEAD_DIM, HEAD_DIM), F32),
            pltpu.VMEM((N_STATES, HEAD_DIM, HEAD_DIM), BF16),
            pltpu.VMEM((N_STATES, HEAD_DIM, V7X_LANES), F32),
            pltpu.VMEM((N_STATES, HEAD_DIM, V7X_LANES), BF16),
            pltpu.VMEM((N_STATES, 1, V7X_LANES), F32),
        ],
        compiler_params=pltpu.CompilerParams(
            dimension_semantics=("parallel", "arbitrary"), vmem_limit_bytes=52 * 2**20),
        name="mlstm",
    )(q4, kt4, v4, rrow, ccol, cs, q4, kt4, v4, rrow, ccol, cs)


def _out_kernel(hf_ref, hb_ref, og_ref, gmb_ref, yc_ref, x_ref, gate_ref, mhg_ref, wpm_ref,
                wout_ref, fg_ref, o_ref, *rest, final):
    ym_scr = rest[-1]
    ts = x_ref.shape[1]
    for hh in range(N_HEADS):
        hs = slice(hh * HEAD_DIM, (hh + 1) * HEAD_DIM)
        hm = hf_ref[0, :, hs].astype(F32) + hb_ref[0, :, hs].astype(F32)
        hm = hm * _rms_scale(hm) * mhg_ref[0:1, hs]
        ym_scr[:, hs] = (og_ref[0, :, hs].astype(F32) * hm).astype(BF16)
    ym = jnp.dot(ym_scr[...], wpm_ref[...], preferred_element_type=F32)
    merged = yc_ref[0].astype(F32) + gmb_ref[0].astype(F32) * ym
    y = jnp.dot(merged.astype(BF16), wout_ref[...], preferred_element_type=F32)
    x_new = x_ref[0] + gate_ref[0] * y
    if final:
        o_ref[0] = x_new * _rms_scale(x_new) * fg_ref[...]
    else:
        o_ref[0] = x_new
        _write_edges(rest[0], x_new[0:1], x_new[ts - 1:ts])


def _out(hf, hb, og, gmb, yc, x, gate, mh_g, w_pm, w_out, final_g, final):
    bsz, seq, _ = x.shape
    ts = PROJ_TILE
    n_tiles = seq // ts
    tok = pl.BlockSpec((1, ts, D_MODEL), lambda b, i: (b, i, 0))
    const = lambda shape: pl.BlockSpec(shape, lambda b, i: (0,) * len(shape))
    out_specs = [tok]
    out_shape = [jax.ShapeDtypeStruct(x.shape, F32)]
    if not final:
        out_specs.append(pl.BlockSpec((1, 1, EDGE_ROWS, D_MODEL), lambda b, i: (b, i, 0, 0)))
        out_shape.append(jax.ShapeDtypeStruct((bsz, n_tiles, EDGE_ROWS, D_MODEL), F32))
    return pl.pallas_call(
        functools.partial(_out_kernel, final=final),
        grid=(bsz, n_tiles),
        in_specs=[tok, tok, tok, tok, tok, tok,
                  pl.BlockSpec((1, 1, D_MODEL), lambda b, i: (b, 0, 0)),
                  const((1, D_MODEL)), const((D_MODEL, D_MODEL)), const((D_MODEL, D_MODEL)),
                  const((1, D_MODEL))],
        out_specs=out_specs,
        out_shape=out_shape,
        scratch_shapes=[pltpu.VMEM((ts, D_MODEL), BF16)],
        compiler_params=pltpu.CompilerParams(
            dimension_semantics=("parallel", "parallel"), vmem_limit_bytes=48 * 2**20),
        name="out",
    )(hf, hb, og, gmb, yc, x, gate, mh_g, w_pm, w_out, final_g)


def kernel(x_prompt, x_sample, c_prompt, c_sample, w_ada, b_ada, norm_g, w_in, b_gates, conv_w,
           conv_b, mh_norm_g, w_proj_conv, w_proj_mlstm, w_out, final_norm_g):
    depth = w_ada.shape[0]
    xs = [x_prompt, x_sample]
    row_slices, row0 = [], 0
    for x in xs:
        row_slices.append(slice(row0, row0 + x.shape[0]))
        row0 += x.shape[0]
    a_all, shift_all, gate_all = _ada(jnp.concatenate([c_prompt, c_sample], axis=0),
                                      w_ada, b_ada, norm_g)
    perm = jnp.array(GATE_PERM)
    final_g = final_norm_g.reshape(1, D_MODEL)
    edges = [_edges(x) for x in xs]

    for l in range(depth):
        w = w_in[l]
        cb_, cc_, cx_, cz_, q_, k_, v_, o_, mz_ = (
            w[:, i * D_MODEL:(i + 1) * D_MODEL] for i in range(9))
        w_g = w[:, 9 * D_MODEL:9 * D_MODEL + N_GATES][:, perm]
        w_gm = w[:, 9 * D_MODEL + N_GATES:]
        w_a = jnp.concatenate([cb_, cc_, cx_, cz_, q_, v_, o_, mz_, w_gm], axis=1).astype(BF16)
        w_kg = jnp.concatenate([k_ * (HEAD_DIM ** -0.5), w_g], axis=1).T.astype(BF16)
        w_pc = w_proj_conv[l].astype(BF16)
        w_pm = w_proj_mlstm[l].astype(BF16)
        w_o = w_out[l].astype(BF16)
        b_g = b_gates[l].reshape(N_GATES)[perm].reshape(N_GATES, 1)
        cbias = conv_b[l].reshape(1, D_MODEL)
        mh_g = mh_norm_g[l].reshape(1, D_MODEL)
        final = l == depth - 1

        new_xs, new_edges = [], []
        for x, edge, rows in zip(xs, edges, row_slices):
            nb = x.shape[0]
            a = a_all[l, rows].reshape(nb, 1, D_MODEL)
            s = shift_all[l, rows].reshape(nb, 1, D_MODEL)
            gate = gate_all[l, rows].reshape(nb, 1, D_MODEL)
            u_edges = _halo(edge, a, s, w_a)
            q4, kt4, v4, og, gmb, yc, rrow, ccol, cs = _proj(
                x, u_edges, a, s, w_a, w_kg, w_pc, conv_w[l], cbias, b_g)
            hf, hb = _mlstm(q4, kt4, v4, rrow, ccol, cs)
            res = _out(hf.reshape(x.shape), hb.reshape(x.shape), og, gmb, yc, x, gate,
                       mh_g, w_pm, w_o, final_g, final=final)
            new_xs.append(res[0])
            new_edges.append(None if final else res[1])
        xs, edges = new_xs, new_edges
    return tuple(xs)
```

```python
import functools
import math

import jax
import jax.numpy as jnp
from jax import lax
from jax.experimental import pallas as pl
from jax.experimental.pallas import tpu as pltpu

F32 = jnp.float32
BF16 = jnp.bfloat16

D_MODEL = 1024
N_HEADS = 4
HEAD_DIM = D_MODEL // N_HEADS
N_GATES = 16
N_STATES = 2 * N_HEADS
EPS = 1e-6
NEG_BIG = -1e30

V7X_LANES = 128
V7X_SUBLANES = 8
V7X_MXU_WIDTH = 256

OFF_CB, OFF_CC, OFF_CX, OFF_CZ = 0, 1 * D_MODEL, 2 * D_MODEL, 3 * D_MODEL
OFF_Q, OFF_V, OFF_GA = 4 * D_MODEL, 5 * D_MODEL, 6 * D_MODEL
W_A_COLS = 7 * D_MODEL
OFF_O, OFF_MZ, OFF_GB = 0, 1 * D_MODEL, 2 * D_MODEL
W_B_COLS = 3 * D_MODEL

GATE_PERM = (0, 1, 2, 3, 8, 9, 10, 11, 4, 5, 6, 7, 12, 13, 14, 15)

PROJ_TILE = 512
NORM_ROWS = 64
COL_BLOCK = 256
MCHUNK = 256
MLSTM_CHUNKS = 4
EDGE_ROWS = V7X_SUBLANES
HALO_BATCH = 8


def _sigmoid(x):
    return jax.nn.sigmoid(x)


def _rms_scale(x):
    return lax.rsqrt(jnp.mean(x * x, axis=-1, keepdims=True) + EPS)


def _twice(x):
    return jnp.concatenate([x, x], axis=1)


def _modulated_norm(x_ref, a_ref, s_ref, h_scr):
    for r0 in range(0, x_ref.shape[1], NORM_ROWS):
        rows = slice(r0, r0 + NORM_ROWS)
        x = x_ref[0, rows, :]
        h_scr[rows, :] = (x * _rms_scale(x) * a_ref[0] + s_ref[0]).astype(BF16)


def _ada_kernel(c_ref, w_ref, b_ref, g_ref, a_ref, shift_ref, gate_ref):
    mod = jnp.dot(c_ref[...], w_ref[0], preferred_element_type=F32,
                  precision=lax.Precision.HIGHEST) + b_ref[0]
    shift_ref[0] = mod[:, :D_MODEL]
    a_ref[0] = g_ref[0] * (1.0 + mod[:, D_MODEL:2 * D_MODEL])
    gate_ref[0] = mod[:, 2 * D_MODEL:]


def _ada(c_all, w_ada, b_ada, norm_g):
    depth = w_ada.shape[0]
    rows = c_all.shape[0]
    out = jax.ShapeDtypeStruct((depth, rows, D_MODEL), F32)
    vec = pl.BlockSpec((1, rows, D_MODEL), lambda l: (l, 0, 0))
    return pl.pallas_call(
        _ada_kernel,
        grid=(depth,),
        in_specs=[
            pl.BlockSpec((rows, D_MODEL), lambda l: (0, 0)),
            pl.BlockSpec((1, D_MODEL, 3 * D_MODEL), lambda l: (l, 0, 0)),
            pl.BlockSpec((1, 1, 3 * D_MODEL), lambda l: (l, 0, 0)),
            pl.BlockSpec((1, 1, D_MODEL), lambda l: (l, 0, 0)),
        ],
        out_specs=[vec, vec, vec],
        out_shape=[out, out, out],
        compiler_params=pltpu.CompilerParams(
            dimension_semantics=("arbitrary",), vmem_limit_bytes=40 * 2**20),
        name="ada",
    )(c_all, w_ada, b_ada.reshape(depth, 1, 3 * D_MODEL), norm_g.reshape(depth, 1, D_MODEL))


def _wkg_kernel(wk_ref, wg_ref, o_ref):
    o_ref[0, :D_MODEL, :] = (wk_ref[0] * (HEAD_DIM ** -0.5)).T.astype(BF16)
    gates_t = wg_ref[0].T
    o_ref[0, D_MODEL:, :] = jnp.concatenate(
        [gates_t[g:g + 1] for g in GATE_PERM], axis=0).astype(BF16)


def _wkg(w_in):
    depth = w_in.shape[0]
    k_block = 5
    gate_block = 9 * D_MODEL // V7X_LANES
    return pl.pallas_call(
        _wkg_kernel,
        grid=(depth,),
        in_specs=[pl.BlockSpec((1, D_MODEL, D_MODEL), lambda l: (l, 0, k_block)),
                  pl.BlockSpec((1, D_MODEL, V7X_LANES), lambda l: (l, 0, gate_block))],
        out_specs=pl.BlockSpec((1, D_MODEL + N_GATES, D_MODEL), lambda l: (l, 0, 0)),
        out_shape=jax.ShapeDtypeStruct((depth, D_MODEL + N_GATES, D_MODEL), BF16),
        compiler_params=pltpu.CompilerParams(dimension_semantics=("parallel",)),
        name="wkg",
    )(w_in, w_in)


def _edges_kernel(first_ref, last_ref, e_ref):
    n_tiles = e_ref.shape[1]
    e_ref[0, :, 0:1, :] = first_ref[0, :, 0:1, :]
    e_ref[0, :, 1:2, :] = last_ref[0, :, V7X_SUBLANES - 1:V7X_SUBLANES, :]
    e_ref[0, :, 2:EDGE_ROWS, :] = jnp.zeros((n_tiles, EDGE_ROWS - 2, D_MODEL), F32)


def _edges(x):
    bsz, seq, _ = x.shape
    n_tiles = seq // PROJ_TILE
    groups = PROJ_TILE // V7X_SUBLANES
    x4 = x.reshape(bsz, n_tiles, PROJ_TILE, D_MODEL)
    group = lambda g: pl.BlockSpec((1, n_tiles, V7X_SUBLANES, D_MODEL),
                                   lambda b: (b, 0, g, 0))
    return pl.pallas_call(
        _edges_kernel,
        grid=(bsz,),
        in_specs=[group(0), group(groups - 1)],
        out_specs=pl.BlockSpec((1, n_tiles, EDGE_ROWS, D_MODEL), lambda b: (b, 0, 0, 0)),
        out_shape=jax.ShapeDtypeStruct((bsz, n_tiles, EDGE_ROWS, D_MODEL), F32),
        compiler_params=pltpu.CompilerParams(dimension_semantics=("parallel",)),
        name="edges",
    )(x4, x4)


def _halo_kernel(x_ref, a_ref, s_ref, wcc_ref, wcx_ref, u_ref):
    nb, rows, _ = x_ref.shape
    x = x_ref[...]
    h = (x * _rms_scale(x) * a_ref[...] + s_ref[...]).astype(BF16).reshape(nb * rows, D_MODEL)
    cc = jnp.dot(h, wcc_ref[...], preferred_element_type=F32)
    cx = jnp.dot(h, wcx_ref[...], preferred_element_type=F32)
    u_ref[...] = (cc * cx).reshape(nb, rows, D_MODEL)


def _halo(edges, a, s, w_a):
    bsz, n_tiles = edges.shape[:2]
    rows = n_tiles * EDGE_ROWS
    nb = math.gcd(bsz, HALO_BATCH)
    blk = pl.BlockSpec((nb, rows, D_MODEL), lambda b: (b, 0, 0))
    per_row = pl.BlockSpec((nb, 1, D_MODEL), lambda b: (b, 0, 0))
    u = pl.pallas_call(
        _halo_kernel,
        grid=(bsz // nb,),
        in_specs=[blk, per_row, per_row,
                  pl.BlockSpec((D_MODEL, D_MODEL), lambda b: (0, OFF_CC // D_MODEL)),
                  pl.BlockSpec((D_MODEL, D_MODEL), lambda b: (0, OFF_CX // D_MODEL))],
        out_specs=blk,
        out_shape=jax.ShapeDtypeStruct((bsz, rows, D_MODEL), F32),
        compiler_params=pltpu.CompilerParams(dimension_semantics=("parallel",)),
        name="halo",
    )(edges.reshape(bsz, rows, D_MODEL), a, s, w_a, w_a)
    return u.reshape(edges.shape)


def _lane_scan(x, lane, op, fill, suffix):
    width = x.shape[1]
    s = 1
    while s < width:
        if suffix:
            shifted = jnp.where(lane < width - s, pltpu.roll(x, width - s, 1), fill)
        else:
            shifted = jnp.where(lane >= s, pltpu.roll(x, s, 1), fill)
        x = op(x, shifted)
        s *= 2
    return x


def _proj_kernel(x_ref, up_ref, un_ref, a_ref, s_ref, wa_ref, wkg_ref, wpc_ref, cw_ref, cbias_ref,
                 bg_ref, q_ref, kt_ref, v_ref, yc_ref, rrow_ref, ccol_ref, cs_ref,
                 h_scr, yconv_scr):
    ts = x_ref.shape[1]
    n_chunks = ts // MCHUNK
    tile = pl.program_id(1)
    _modulated_norm(x_ref, a_ref, s_ref, h_scr)

    def mm(col, width=COL_BLOCK):
        return jnp.dot(h_scr[...], wa_ref[:, col:col + width], preferred_element_type=F32)

    def qv_block(c0):
        cols = slice(c0, c0 + COL_BLOCK)
        q_ref[0, :, :, cols] = mm(OFF_Q + c0).astype(BF16).reshape(n_chunks, MCHUNK, COL_BLOCK)
        v_ref[0, :, :, cols] = mm(OFF_V + c0).astype(BF16).reshape(n_chunks, MCHUNK, COL_BLOCK)

    qv_block(0)

    lane = lax.broadcasted_iota(jnp.int32, (N_STATES, MCHUNK), 1)
    is_fwd = lax.broadcasted_iota(jnp.int32, (N_STATES, MCHUNK), 0) < N_HEADS
    pad_rows = jnp.zeros((V7X_LANES - 2 * N_STATES, V7X_LANES), F32)
    for c in range(n_chunks):
        kg = lax.dot_general(wkg_ref[0], h_scr[c * MCHUNK:(c + 1) * MCHUNK, :],
                             (((1,), (1,)), ((), ())),
                             preferred_element_type=F32)
        kt_ref[0, c] = kg[:D_MODEL].astype(BF16)

        gates = kg[D_MODEL:] + bg_ref[...]
        lf = jax.nn.log_sigmoid(gates[N_STATES:])
        b = jnp.where(is_fwd, _lane_scan(lf, lane, jnp.add, 0.0, False),
                      _lane_scan(lf, lane, jnp.add, 0.0, True))
        r = gates[:N_STATES] - b
        cmr = jnp.where(is_fwd, _lane_scan(r, lane, jnp.maximum, NEG_BIG, False),
                        _lane_scan(r, lane, jnp.maximum, NEG_BIG, True))
        rrow_ref[0, c] = r
        cs_ref[0, c, 0:N_STATES, :] = jnp.broadcast_to(
            jnp.max(r, axis=1, keepdims=True), (N_STATES, V7X_LANES))
        cs_ref[0, c, N_STATES:, :] = jnp.broadcast_to(
            jnp.sum(lf, axis=1, keepdims=True), (N_STATES, V7X_LANES))
        for half in range(MCHUNK // V7X_LANES):
            ls = slice(half * V7X_LANES, (half + 1) * V7X_LANES)
            cols_t = jnp.concatenate([cmr[:, ls], b[:, ls], pad_rows], axis=0).T
            ccol_ref[0, c, ls, :] = cols_t[:, :2 * N_STATES]

    row = lax.broadcasted_iota(jnp.int32, (ts, COL_BLOCK), 0)
    has_prev = tile > 0
    has_next = tile < pl.num_programs(1) - 1
    for c0 in range(0, D_MODEL, COL_BLOCK):
        cols = slice(c0, c0 + COL_BLOCK)
        u = mm(OFF_CC + c0) * mm(OFF_CX + c0)
        edge_prev = jnp.where(has_prev, up_ref[0, 0, 1:2, cols], 0.0)
        edge_next = jnp.where(has_next, un_ref[0, 0, 0:1, cols], 0.0)
        u_prev = jnp.where(row == 0, edge_prev, pltpu.roll(u, 1, 0))
        u_next = jnp.where(row == ts - 1, edge_next, pltpu.roll(u, ts - 1, 0))
        conv = (u_prev * cw_ref[0:1, cols] + u * cw_ref[1:2, cols] + u_next * cw_ref[2:3, cols]
                + cbias_ref[0:1, cols])
        cz = mm(OFF_CZ + c0)
        y = mm(OFF_CB + c0) * conv * (cz * _sigmoid(cz))
        yconv_scr[:, cols] = y.astype(BF16)

    for c0 in range(0, D_MODEL, COL_BLOCK):
        cols = slice(c0, c0 + COL_BLOCK)
        yc = jnp.dot(yconv_scr[...], wpc_ref[:, cols], preferred_element_type=F32)
        yc_ref[0, :, cols] = (_sigmoid(mm(OFF_GA + c0)) * yc).astype(BF16)
        if c0 > 0:
            qv_block(c0)


def _proj(x, u_edges, a, s, w_a, w_kg_all, layer, w_pc, conv_w, conv_b, b_g):
    bsz, seq, _ = x.shape
    ts = PROJ_TILE
    n_tiles = seq // ts
    n_chunks = seq // MCHUNK
    cps = ts // MCHUNK

    def const(shape):
        return pl.BlockSpec(shape, lambda b, i: (0,) * len(shape), pipeline_mode=pl.Buffered(1))

    tok = pl.BlockSpec((1, ts, D_MODEL), lambda b, i: (b, i, 0))
    per_row = pl.BlockSpec((1, 1, D_MODEL), lambda b, i: (b, 0, 0))
    edge = lambda index: pl.BlockSpec((1, 1, EDGE_ROWS, D_MODEL),
                                      lambda b, i: (b, index(i), 0, 0))
    chunked = lambda r, c: pl.BlockSpec((1, cps, r, c), lambda b, i: (b, i, 0, 0))
    return pl.pallas_call(
        _proj_kernel,
        grid=(bsz, n_tiles),
        in_specs=[
            tok,
            edge(lambda i: jnp.maximum(i - 1, 0)), edge(lambda i: jnp.minimum(i + 1, n_tiles - 1)),
            per_row, per_row,
            const((D_MODEL, W_A_COLS)),
            pl.BlockSpec((1, D_MODEL + N_GATES, D_MODEL), lambda b, i: (layer, 0, 0),
                         pipeline_mode=pl.Buffered(1)),
            const((D_MODEL, D_MODEL)), const((3, D_MODEL)), const((1, D_MODEL)),
            const((N_GATES, 1)),
        ],
        out_specs=[
            chunked(MCHUNK, D_MODEL), chunked(D_MODEL, MCHUNK), chunked(MCHUNK, D_MODEL),
            tok,
            chunked(N_STATES, MCHUNK), chunked(MCHUNK, 2 * N_STATES),
            chunked(2 * N_STATES, V7X_LANES),
        ],
        out_shape=[
            jax.ShapeDtypeStruct((bsz, n_chunks, MCHUNK, D_MODEL), BF16),
            jax.ShapeDtypeStruct((bsz, n_chunks, D_MODEL, MCHUNK), BF16),
            jax.ShapeDtypeStruct((bsz, n_chunks, MCHUNK, D_MODEL), BF16),
            jax.ShapeDtypeStruct((bsz, seq, D_MODEL), BF16),
            jax.ShapeDtypeStruct((bsz, n_chunks, N_STATES, MCHUNK), F32),
            jax.ShapeDtypeStruct((bsz, n_chunks, MCHUNK, 2 * N_STATES), F32),
            jax.ShapeDtypeStruct((bsz, n_chunks, 2 * N_STATES, V7X_LANES), F32),
        ],
        scratch_shapes=[pltpu.VMEM((ts, D_MODEL), BF16), pltpu.VMEM((ts, D_MODEL), BF16)],
        compiler_params=pltpu.CompilerParams(
            dimension_semantics=("parallel", "parallel"), vmem_limit_bytes=52 * 2**20),
        name="proj",
    )(x, u_edges, u_edges, a, s, w_a, w_kg_all, w_pc, conv_w, conv_b, b_g)


def _mlstm_kernel(qf_ref, ktf_ref, vf_ref, rf_ref, cf_ref, sf_ref,
                  qb_ref, ktb_ref, vb_ref, rb_ref, cb_ref, sb_ref,
                  hf_ref, hb_ref,
                  c_scr, cbf_scr, n_scr, nbf_scr, m_scr):
    cps = qf_ref.shape[1]

    @pl.when(pl.program_id(1) == 0)
    def _():
        c_scr[...] = jnp.zeros_like(c_scr)
        cbf_scr[...] = jnp.zeros_like(cbf_scr)
        n_scr[...] = jnp.zeros_like(n_scr)
        nbf_scr[...] = jnp.zeros_like(nbf_scr)
        m_scr[...] = jnp.zeros_like(m_scr)

    t_idx = lax.broadcasted_iota(jnp.int32, (MCHUNK, MCHUNK), 0)
    s_idx = lax.broadcasted_iota(jnp.int32, (MCHUNK, MCHUNK), 1)
    masks = (s_idx <= t_idx, s_idx >= t_idx)
    ones = jnp.ones((MCHUNK, V7X_LANES), BF16)
    dirs = ((qf_ref, ktf_ref, vf_ref, rf_ref, cf_ref, sf_ref, hf_ref),
            (qb_ref, ktb_ref, vb_ref, rb_ref, cb_ref, sb_ref, hb_ref))

    def chunk_body(step, carry):
        for d, (q_ref, kt_ref, v_ref, r_ref, col_ref, cs_ref, h_ref) in enumerate(dirs):
            c = step if d == 0 else cps - 1 - step
            for hh in range(N_HEADS):
                i = d * N_HEADS + hh
                hs = slice(hh * HEAD_DIM, (hh + 1) * HEAD_DIM)
                q = q_ref[0, c, :, hs]
                kt = kt_ref[0, c, hs, :]
                v = v_ref[0, c, :, hs]
                r_row = r_ref[0, c, i:i + 1, :]
                cmr = jnp.broadcast_to(col_ref[0, c, :, i:i + 1], (MCHUNK, V7X_LANES))
                b_cum = jnp.broadcast_to(col_ref[0, c, :, N_STATES + i:N_STATES + i + 1],
                                         (MCHUNK, V7X_LANES))
                r_max = cs_ref[0, c, i:i + 1, :]
                b_tot = cs_ref[0, c, N_STATES + i:N_STATES + i + 1, :]
                m_row = m_scr[i]

                big_m = jnp.maximum(cmr, m_row)
                d_exp = jnp.exp(jnp.where(masks[d], r_row - _twice(big_m), NEG_BIG))
                p = jnp.dot(q, kt, preferred_element_type=F32) * d_exp
                p_bf = p.astype(BF16)
                inter_w = jnp.exp(m_row - big_m)
                q_n = jnp.dot(q, nbf_scr[i], preferred_element_type=F32)
                p_sum = jnp.dot(p_bf, ones, preferred_element_type=F32)
                den = jnp.maximum(jnp.abs(p_sum + inter_w * q_n), jnp.exp(-(b_cum + big_m)))
                inv = 1.0 / den
                q_c = jnp.dot(q, cbf_scr[i], preferred_element_type=F32)
                num = jnp.dot(p_bf, v, preferred_element_type=F32) + _twice(inter_w) * q_c
                h_ref[0, c, :, hs] = (num * _twice(inv)).astype(BF16)

                m_x = jnp.maximum(m_row, r_max)
                w_row = jnp.exp(r_row - _twice(m_x))
                decay = jnp.exp(m_row - m_x)
                ktw = kt * w_row.astype(BF16)
                c_new = _twice(decay) * c_scr[i] + jnp.dot(ktw, v, preferred_element_type=F32)
                c_scr[i] = c_new
                cbf_scr[i] = c_new.astype(BF16)
                n_new = decay * n_scr[i] + jnp.dot(ktw, ones, preferred_element_type=F32)
                n_scr[i] = n_new
                nbf_scr[i] = n_new.astype(BF16)
                m_scr[i] = b_tot + m_x
        return carry

    lax.fori_loop(0, cps, chunk_body, 0, unroll=2)


def _mlstm(q4, kt4, v4, rrow, ccol, cs):
    bsz, n_chunks = q4.shape[:2]
    cps = min(MLSTM_CHUNKS, n_chunks)
    n_steps = n_chunks // cps

    def specs(index):
        blk = lambda r, c: pl.BlockSpec((1, cps, r, c), lambda b, j: (b, index(j), 0, 0))
        return [blk(MCHUNK, D_MODEL), blk(D_MODEL, MCHUNK), blk(MCHUNK, D_MODEL),
                blk(N_STATES, MCHUNK), blk(MCHUNK, 2 * N_STATES), blk(2 * N_STATES, V7X_LANES)]

    fwd = lambda j: j
    bwd = lambda j: n_steps - 1 - j
    h_shape = jax.ShapeDtypeStruct(q4.shape, BF16)
    return pl.pallas_call(
        _mlstm_kernel,
        grid=(bsz, n_steps),
        in_specs=specs(fwd) + specs(bwd),
        out_specs=[specs(fwd)[0], specs(bwd)[0]],
        out_shape=[h_shape, h_shape],
        scratch_shapes=[
            pltpu.VMEM((N_STATES, HEAD_DIM, HEAD_DIM), F32),
            pltpu.VMEM((N_STATES, HEAD_DIM, HEAD_DIM), BF16),
            pltpu.VMEM((N_STATES, HEAD_DIM, V7X_LANES), F32),
            pltpu.VMEM((N_STATES, HEAD_DIM, V7X_LANES), BF16),
            pltpu.VMEM((N_STATES, 1, V7X_LANES), F32),
        ],
        compiler_params=pltpu.CompilerParams(
            dimension_semantics=("parallel", "arbitrary"), vmem_limit_bytes=52 * 2**20),
        name="mlstm",
    )(q4, kt4, v4, rrow, ccol, cs, q4, kt4, v4, rrow, ccol, cs)


def _out_kernel(hf_ref, hb_ref, yc_ref, x_ref, a_ref, s_ref, gate_ref, mhg_ref, wb_ref, wpm_ref,
                wout_ref, fg_ref, o_ref, *rest, final):
    h_scr, ym_scr, merged_scr = rest[-3:]
    ts = x_ref.shape[1]
    _modulated_norm(x_ref, a_ref, s_ref, h_scr)

    def mm(col):
        return jnp.dot(h_scr[...], wb_ref[:, col:col + COL_BLOCK], preferred_element_type=F32)

    for hh in range(N_HEADS):
        hs = slice(hh * HEAD_DIM, (hh + 1) * HEAD_DIM)
        mz = mm(OFF_MZ + hh * HEAD_DIM)
        og = _sigmoid(mm(OFF_O + hh * HEAD_DIM)) * (mz * _sigmoid(mz))
        hm = hf_ref[0, :, hs].astype(F32) + hb_ref[0, :, hs].astype(F32)
        hm = hm * _rms_scale(hm) * mhg_ref[0:1, hs]
        ym_scr[:, hs] = (og * hm).astype(BF16)
    for c0 in range(0, D_MODEL, COL_BLOCK):
        cols = slice(c0, c0 + COL_BLOCK)
        ym = jnp.dot(ym_scr[...], wpm_ref[:, cols], preferred_element_type=F32)
        merged = yc_ref[0, :, cols].astype(F32) + _sigmoid(mm(OFF_GB + c0)) * ym
        merged_scr[:, cols] = merged.astype(BF16)
    for c0 in range(0, D_MODEL, COL_BLOCK):
        cols = slice(c0, c0 + COL_BLOCK)
        y = jnp.dot(merged_scr[...], wout_ref[:, cols], preferred_element_type=F32)
        o_ref[0, :, cols] = x_ref[0, :, cols] + gate_ref[0, :, cols] * y
    if final:
        x_new = o_ref[0]
        o_ref[0] = x_new * _rms_scale(x_new) * fg_ref[...]
    else:
        e_ref = rest[0]
        e_ref[0, 0, 0:1, :] = o_ref[0, 0:1, :]
        e_ref[0, 0, 1:2, :] = o_ref[0, ts - 1:ts, :]
        e_ref[0, 0, 2:EDGE_ROWS, :] = jnp.zeros((EDGE_ROWS - 2, D_MODEL), F32)


def _out(hf, hb, yc, x, a, s, gate, mh_g, w_b, w_pm, w_out, final_g, final):
    bsz, seq, _ = x.shape
    ts = PROJ_TILE
    n_tiles = seq // ts
    tok = pl.BlockSpec((1, ts, D_MODEL), lambda b, i: (b, i, 0))
    per_row = pl.BlockSpec((1, 1, D_MODEL), lambda b, i: (b, 0, 0))

    def const(shape):
        return pl.BlockSpec(shape, lambda b, i: (0,) * len(shape), pipeline_mode=pl.Buffered(1))

    out_specs = [tok]
    out_shape = [jax.ShapeDtypeStruct(x.shape, F32)]
    if not final:
        out_specs.append(pl.BlockSpec((1, 1, EDGE_ROWS, D_MODEL), lambda b, i: (b, i, 0, 0)))
        out_shape.append(jax.ShapeDtypeStruct((bsz, n_tiles, EDGE_ROWS, D_MODEL), F32))
    tile_bf16 = pltpu.VMEM((ts, D_MODEL), BF16)
    return pl.pallas_call(
        functools.partial(_out_kernel, final=final),
        grid=(bsz, n_tiles),
        in_specs=[tok, tok, tok, tok, per_row, per_row, per_row,
                  const((1, D_MODEL)), const((D_MODEL, W_B_COLS)), const((D_MODEL, D_MODEL)),
                  const((D_MODEL, D_MODEL)), const((1, D_MODEL))],
        out_specs=out_specs,
        out_shape=out_shape,
        scratch_shapes=[tile_bf16, tile_bf16, tile_bf16],
        compiler_params=pltpu.CompilerParams(
            dimension_semantics=("parallel", "parallel"), vmem_limit_bytes=48 * 2**20),
        name="out",
    )(hf, hb, yc, x, a, s, gate, mh_g, w_b, w_pm, w_out, final_g)


def kernel(x_prompt, x_sample, c_prompt, c_sample, w_ada, b_ada, norm_g, w_in, b_gates, conv_w,
           conv_b, mh_norm_g, w_proj_conv, w_proj_mlstm, w_out, final_norm_g):
    depth = w_ada.shape[0]
    xs = [x_prompt, x_sample]
    row_slices, row0 = [], 0
    for x in xs:
        row_slices.append(slice(row0, row0 + x.shape[0]))
        row0 += x.shape[0]
    a_all, shift_all, gate_all = _ada(jnp.concatenate([c_prompt, c_sample], axis=0),
                                      w_ada, b_ada, norm_g)
    perm = jnp.array(GATE_PERM)
    final_g = final_norm_g.reshape(1, D_MODEL)
    edges = [_edges(x) for x in xs]
    w_kg_all = _wkg(w_in)

    for l in range(depth):
        w = w_in[l]
        cb_, cc_, cx_, cz_, q_, _, v_, o_, mz_ = (
            w[:, i * D_MODEL:(i + 1) * D_MODEL] for i in range(9))
        w_ga = w[:, 9 * D_MODEL + N_GATES:10 * D_MODEL + N_GATES]
        w_gb = w[:, 10 * D_MODEL + N_GATES:]
        w_a = jnp.concatenate([cb_, cc_, cx_, cz_, q_, v_, w_ga], axis=1).astype(BF16)
        w_b = jnp.concatenate([o_, mz_, w_gb], axis=1).astype(BF16)
        w_pc = w_proj_conv[l].astype(BF16)
        w_pm = w_proj_mlstm[l].astype(BF16)
        w_o = w_out[l].astype(BF16)
        b_g = b_gates[l].reshape(N_GATES)[perm].reshape(N_GATES, 1)
        cbias = conv_b[l].reshape(1, D_MODEL)
        mh_g = mh_norm_g[l].reshape(1, D_MODEL)
        final = l == depth - 1

        new_xs, new_edges = [], []
        for x, edge, rows in zip(xs, edges, row_slices):
            nb = x.shape[0]
            a = a_all[l, rows].reshape(nb, 1, D_MODEL)
            s = shift_all[l, rows].reshape(nb, 1, D_MODEL)
            gate = gate_all[l, rows].reshape(nb, 1, D_MODEL)
            u_edges = _halo(edge, a, s, w_a)
            q4, kt4, v4, yc, rrow, ccol, cs = _proj(
                x, u_edges, a, s, w_a, w_kg_all, l, w_pc, conv_w[l], cbias, b_g)
            hf, hb = _mlstm(q4, kt4, v4, rrow, ccol, cs)
            res = _out(hf.reshape(x.shape), hb.reshape(x.shape), yc, x, a, s, gate,
                       mh_g, w_b, w_pm, w_o, final_g, final=final)
            new_xs.append(res[0])
            new_edges.append(None if final else res[1])
        xs, edges = new_xs, new_edges
    return tuple(xs)
```

```python
import functools
import math

import jax
import jax.numpy as jnp
from jax import lax
from jax.experimental import pallas as pl
from jax.experimental.pallas import tpu as pltpu

F32 = jnp.float32
BF16 = jnp.bfloat16

D_MODEL = 1024
N_HEADS = 4
HEAD_DIM = D_MODEL // N_HEADS
N_GATES = 16
N_STATES = 2 * N_HEADS
EPS = 1e-6
NEG_BIG = -1e30

V7X_LANES = 128
V7X_SUBLANES = 8
V7X_MXU_WIDTH = 256

OFF_CB, OFF_CC, OFF_CX, OFF_CZ = 0, 1 * D_MODEL, 2 * D_MODEL, 3 * D_MODEL
OFF_Q, OFF_V, OFF_GA = 4 * D_MODEL, 5 * D_MODEL, 6 * D_MODEL
W_A_COLS = 7 * D_MODEL
OFF_O, OFF_MZ, OFF_GB = 0, 1 * D_MODEL, 2 * D_MODEL
W_B_COLS = 3 * D_MODEL

GATE_PERM = (0, 1, 2, 3, 8, 9, 10, 11, 4, 5, 6, 7, 12, 13, 14, 15)

PROJ_TILE = 512
OUT_TILE = 1024
NORM_ROWS = 64
COL_BLOCK = 256
MCHUNK = 256
MLSTM_CHUNKS = 4
EDGE_ROWS = V7X_SUBLANES
HALO_BATCH = 8


def _sigmoid(x):
    return jax.nn.sigmoid(x)


def _rms_scale(x):
    return lax.rsqrt(jnp.mean(x * x, axis=-1, keepdims=True) + EPS)


def _twice(x):
    return jnp.concatenate([x, x], axis=1)


def _modulated_norm(x_ref, a_ref, s_ref, h_ref):
    for r0 in range(0, x_ref.shape[1], NORM_ROWS):
        rows = slice(r0, r0 + NORM_ROWS)
        x = x_ref[0, rows, :]
        h_ref[0, rows, :] = (x * _rms_scale(x) * a_ref[0] + s_ref[0]).astype(BF16)


def _ada_kernel(c_ref, w_ref, b_ref, g_ref, a_ref, shift_ref, gate_ref):
    mod = jnp.dot(c_ref[...], w_ref[0], preferred_element_type=F32,
                  precision=lax.Precision.HIGHEST) + b_ref[0]
    shift_ref[0] = mod[:, :D_MODEL]
    a_ref[0] = g_ref[0] * (1.0 + mod[:, D_MODEL:2 * D_MODEL])
    gate_ref[0] = mod[:, 2 * D_MODEL:]


def _ada(c_all, w_ada, b_ada, norm_g):
    depth = w_ada.shape[0]
    rows = c_all.shape[0]
    out = jax.ShapeDtypeStruct((depth, rows, D_MODEL), F32)
    vec = pl.BlockSpec((1, rows, D_MODEL), lambda l: (l, 0, 0))
    return pl.pallas_call(
        _ada_kernel,
        grid=(depth,),
        in_specs=[
            pl.BlockSpec((rows, D_MODEL), lambda l: (0, 0)),
            pl.BlockSpec((1, D_MODEL, 3 * D_MODEL), lambda l: (l, 0, 0)),
            pl.BlockSpec((1, 1, 3 * D_MODEL), lambda l: (l, 0, 0)),
            pl.BlockSpec((1, 1, D_MODEL), lambda l: (l, 0, 0)),
        ],
        out_specs=[vec, vec, vec],
        out_shape=[out, out, out],
        compiler_params=pltpu.CompilerParams(
            dimension_semantics=("arbitrary",), vmem_limit_bytes=40 * 2**20),
        name="ada",
    )(c_all, w_ada, b_ada.reshape(depth, 1, 3 * D_MODEL), norm_g.reshape(depth, 1, D_MODEL))


def _wkg_kernel(wk_ref, wg_ref, o_ref):
    o_ref[0, :D_MODEL, :] = (wk_ref[0] * (HEAD_DIM ** -0.5)).T.astype(BF16)
    gates_t = wg_ref[0].T
    o_ref[0, D_MODEL:, :] = jnp.concatenate(
        [gates_t[g:g + 1] for g in GATE_PERM], axis=0).astype(BF16)


def _wkg(w_in):
    depth = w_in.shape[0]
    w_k = w_in[:, :, 5 * D_MODEL:6 * D_MODEL]
    w_gates = w_in[:, :, 9 * D_MODEL:9 * D_MODEL + V7X_LANES]
    return pl.pallas_call(
        _wkg_kernel,
        grid=(depth,),
        in_specs=[pl.BlockSpec((1, D_MODEL, D_MODEL), lambda l: (l, 0, 0)),
                  pl.BlockSpec((1, D_MODEL, V7X_LANES), lambda l: (l, 0, 0))],
        out_specs=pl.BlockSpec((1, D_MODEL + N_GATES, D_MODEL), lambda l: (l, 0, 0)),
        out_shape=jax.ShapeDtypeStruct((depth, D_MODEL + N_GATES, D_MODEL), BF16),
        compiler_params=pltpu.CompilerParams(dimension_semantics=("parallel",)),
        name="wkg",
    )(w_k, w_gates)


def _edges_kernel(first_ref, last_ref, e_ref):
    n_tiles = e_ref.shape[1]
    e_ref[0, :, 0:1, :] = first_ref[0, :, 0:1, :]
    e_ref[0, :, 1:2, :] = last_ref[0, :, V7X_SUBLANES - 1:V7X_SUBLANES, :]
    e_ref[0, :, 2:EDGE_ROWS, :] = jnp.zeros((n_tiles, EDGE_ROWS - 2, D_MODEL), F32)


def _edges(x):
    bsz, seq, _ = x.shape
    n_tiles = seq // PROJ_TILE
    groups = PROJ_TILE // V7X_SUBLANES
    x4 = x.reshape(bsz, n_tiles, PROJ_TILE, D_MODEL)
    group = lambda g: pl.BlockSpec((1, n_tiles, V7X_SUBLANES, D_MODEL),
                                   lambda b: (b, 0, g, 0))
    return pl.pallas_call(
        _edges_kernel,
        grid=(bsz,),
        in_specs=[group(0), group(groups - 1)],
        out_specs=pl.BlockSpec((1, n_tiles, EDGE_ROWS, D_MODEL), lambda b: (b, 0, 0, 0)),
        out_shape=jax.ShapeDtypeStruct((bsz, n_tiles, EDGE_ROWS, D_MODEL), F32),
        compiler_params=pltpu.CompilerParams(dimension_semantics=("parallel",)),
        name="edges",
    )(x4, x4)


def _halo_kernel(x_ref, a_ref, s_ref, wcc_ref, wcx_ref, u_ref):
    nb, rows, _ = x_ref.shape
    x = x_ref[...]
    h = (x * _rms_scale(x) * a_ref[...] + s_ref[...]).astype(BF16).reshape(nb * rows, D_MODEL)
    cc = jnp.dot(h, wcc_ref[...], preferred_element_type=F32)
    cx = jnp.dot(h, wcx_ref[...], preferred_element_type=F32)
    u_ref[...] = (cc * cx).reshape(nb, rows, D_MODEL)


def _halo(edges, a, s, w_a):
    bsz, n_tiles = edges.shape[:2]
    rows = n_tiles * EDGE_ROWS
    nb = math.gcd(bsz, HALO_BATCH)
    blk = pl.BlockSpec((nb, rows, D_MODEL), lambda b: (b, 0, 0))
    per_row = pl.BlockSpec((nb, 1, D_MODEL), lambda b: (b, 0, 0))
    u = pl.pallas_call(
        _halo_kernel,
        grid=(bsz // nb,),
        in_specs=[blk, per_row, per_row,
                  pl.BlockSpec((D_MODEL, D_MODEL), lambda b: (0, OFF_CC // D_MODEL)),
                  pl.BlockSpec((D_MODEL, D_MODEL), lambda b: (0, OFF_CX // D_MODEL))],
        out_specs=blk,
        out_shape=jax.ShapeDtypeStruct((bsz, rows, D_MODEL), F32),
        compiler_params=pltpu.CompilerParams(dimension_semantics=("parallel",)),
        name="halo",
    )(edges.reshape(bsz, rows, D_MODEL), a, s, w_a, w_a)
    return u.reshape(edges.shape)


def _lane_scan(x, lane, op, fill, suffix):
    width = x.shape[1]
    s = 1
    while s < width:
        if suffix:
            shifted = jnp.where(lane < width - s, pltpu.roll(x, width - s, 1), fill)
        else:
            shifted = jnp.where(lane >= s, pltpu.roll(x, s, 1), fill)
        x = op(x, shifted)
        s *= 2
    return x


def _proj_kernel(x_ref, up_ref, un_ref, a_ref, s_ref, wa_ref, wkg_ref, wpc_ref, cw_ref, cbias_ref,
                 bg_ref, q_ref, kt_ref, v_ref, yc_ref, h_ref, rrow_ref, ccol_ref, cs_ref,
                 yconv_scr):
    ts = x_ref.shape[1]
    n_chunks = ts // MCHUNK
    tile = pl.program_id(1)
    _modulated_norm(x_ref, a_ref, s_ref, h_ref)

    def mm(col, width=COL_BLOCK):
        return jnp.dot(h_ref[0], wa_ref[:, col:col + width], preferred_element_type=F32)

    def qv_block(c0):
        cols = slice(c0, c0 + COL_BLOCK)
        q_ref[0, :, :, cols] = mm(OFF_Q + c0).astype(BF16).reshape(n_chunks, MCHUNK, COL_BLOCK)
        v_ref[0, :, :, cols] = mm(OFF_V + c0).astype(BF16).reshape(n_chunks, MCHUNK, COL_BLOCK)

    qv_block(0)

    lane = lax.broadcasted_iota(jnp.int32, (N_STATES, MCHUNK), 1)
    is_fwd = lax.broadcasted_iota(jnp.int32, (N_STATES, MCHUNK), 0) < N_HEADS
    pad_rows = jnp.zeros((V7X_LANES - 2 * N_STATES, V7X_LANES), F32)
    for c in range(n_chunks):
        kg = lax.dot_general(wkg_ref[0], h_ref[0, c * MCHUNK:(c + 1) * MCHUNK, :],
                             (((1,), (1,)), ((), ())),
                             preferred_element_type=F32)
        kt_ref[0, c] = kg[:D_MODEL].astype(BF16)

        gates = kg[D_MODEL:] + bg_ref[...]
        lf = jax.nn.log_sigmoid(gates[N_STATES:])
        b = jnp.where(is_fwd, _lane_scan(lf, lane, jnp.add, 0.0, False),
                      _lane_scan(lf, lane, jnp.add, 0.0, True))
        r = gates[:N_STATES] - b
        cmr = jnp.where(is_fwd, _lane_scan(r, lane, jnp.maximum, NEG_BIG, False),
                        _lane_scan(r, lane, jnp.maximum, NEG_BIG, True))
        rrow_ref[0, c] = r
        cs_ref[0, c, 0:N_STATES, :] = jnp.broadcast_to(
            jnp.max(r, axis=1, keepdims=True), (N_STATES, V7X_LANES))
        cs_ref[0, c, N_STATES:, :] = jnp.broadcast_to(
            jnp.sum(lf, axis=1, keepdims=True), (N_STATES, V7X_LANES))
        for half in range(MCHUNK // V7X_LANES):
            ls = slice(half * V7X_LANES, (half + 1) * V7X_LANES)
            cols_t = jnp.concatenate([cmr[:, ls], b[:, ls], pad_rows], axis=0).T
            ccol_ref[0, c, ls, :] = cols_t[:, :2 * N_STATES]

    row = lax.broadcasted_iota(jnp.int32, (ts, COL_BLOCK), 0)
    has_prev = tile > 0
    has_next = tile < pl.num_programs(1) - 1
    for c0 in range(0, D_MODEL, COL_BLOCK):
        cols = slice(c0, c0 + COL_BLOCK)
        u = mm(OFF_CC + c0) * mm(OFF_CX + c0)
        edge_prev = jnp.where(has_prev, up_ref[0, 0, 1:2, cols], 0.0)
        edge_next = jnp.where(has_next, un_ref[0, 0, 0:1, cols], 0.0)
        u_prev = jnp.where(row == 0, edge_prev, pltpu.roll(u, 1, 0))
        u_next = jnp.where(row == ts - 1, edge_next, pltpu.roll(u, ts - 1, 0))
        conv = (u_prev * cw_ref[0:1, cols] + u * cw_ref[1:2, cols] + u_next * cw_ref[2:3, cols]
                + cbias_ref[0:1, cols])
        cz = mm(OFF_CZ + c0)
        y = mm(OFF_CB + c0) * conv * (cz * _sigmoid(cz))
        yconv_scr[:, cols] = y.astype(BF16)

    for c0 in range(0, D_MODEL, COL_BLOCK):
        cols = slice(c0, c0 + COL_BLOCK)
        yc = jnp.dot(yconv_scr[...], wpc_ref[:, cols], preferred_element_type=F32)
        yc_ref[0, :, cols] = (_sigmoid(mm(OFF_GA + c0)) * yc).astype(BF16)
        if c0 > 0:
            qv_block(c0)


def _proj(x, u_edges, a, s, w_a, w_kg_all, layer, w_pc, conv_w, conv_b, b_g):
    bsz, seq, _ = x.shape
    ts = PROJ_TILE
    n_tiles = seq // ts
    n_chunks = seq // MCHUNK
    cps = ts // MCHUNK

    def const(shape):
        return pl.BlockSpec(shape, lambda b, i: (0,) * len(shape), pipeline_mode=pl.Buffered(1))

    tok = pl.BlockSpec((1, ts, D_MODEL), lambda b, i: (b, i, 0))
    per_row = pl.BlockSpec((1, 1, D_MODEL), lambda b, i: (b, 0, 0))
    edge = lambda index: pl.BlockSpec((1, 1, EDGE_ROWS, D_MODEL),
                                      lambda b, i: (b, index(i), 0, 0))
    chunked = lambda r, c: pl.BlockSpec((1, cps, r, c), lambda b, i: (b, i, 0, 0))
    return pl.pallas_call(
        _proj_kernel,
        grid=(bsz, n_tiles),
        in_specs=[
            tok,
            edge(lambda i: jnp.maximum(i - 1, 0)), edge(lambda i: jnp.minimum(i + 1, n_tiles - 1)),
            per_row, per_row,
            const((D_MODEL, W_A_COLS)),
            pl.BlockSpec((1, D_MODEL + N_GATES, D_MODEL), lambda b, i: (layer, 0, 0),
                         pipeline_mode=pl.Buffered(1)),
            const((D_MODEL, D_MODEL)), const((3, D_MODEL)), const((1, D_MODEL)),
            const((N_GATES, 1)),
        ],
        out_specs=[
            chunked(MCHUNK, D_MODEL), chunked(D_MODEL, MCHUNK), chunked(MCHUNK, D_MODEL),
            tok, tok,
            chunked(N_STATES, MCHUNK), chunked(MCHUNK, 2 * N_STATES),
            chunked(2 * N_STATES, V7X_LANES),
        ],
        out_shape=[
            jax.ShapeDtypeStruct((bsz, n_chunks, MCHUNK, D_MODEL), BF16),
            jax.ShapeDtypeStruct((bsz, n_chunks, D_MODEL, MCHUNK), BF16),
            jax.ShapeDtypeStruct((bsz, n_chunks, MCHUNK, D_MODEL), BF16),
            jax.ShapeDtypeStruct((bsz, seq, D_MODEL), BF16),
            jax.ShapeDtypeStruct((bsz, seq, D_MODEL), BF16),
            jax.ShapeDtypeStruct((bsz, n_chunks, N_STATES, MCHUNK), F32),
            jax.ShapeDtypeStruct((bsz, n_chunks, MCHUNK, 2 * N_STATES), F32),
            jax.ShapeDtypeStruct((bsz, n_chunks, 2 * N_STATES, V7X_LANES), F32),
        ],
        scratch_shapes=[pltpu.VMEM((ts, D_MODEL), BF16)],
        compiler_params=pltpu.CompilerParams(
            dimension_semantics=("parallel", "parallel"), vmem_limit_bytes=52 * 2**20),
        name="proj",
    )(x, u_edges, u_edges, a, s, w_a, w_kg_all, w_pc, conv_w, conv_b, b_g)


def _mlstm_kernel(qf_ref, ktf_ref, vf_ref, rf_ref, cf_ref, sf_ref,
                  qb_ref, ktb_ref, vb_ref, rb_ref, cb_ref, sb_ref,
                  hf_ref, hb_ref,
                  c_scr, cbf_scr, n_scr, nbf_scr, m_scr):
    cps = qf_ref.shape[1]

    @pl.when(pl.program_id(1) == 0)
    def _():
        c_scr[...] = jnp.zeros_like(c_scr)
        cbf_scr[...] = jnp.zeros_like(cbf_scr)
        n_scr[...] = jnp.zeros_like(n_scr)
        nbf_scr[...] = jnp.zeros_like(nbf_scr)
        m_scr[...] = jnp.zeros_like(m_scr)

    t_idx = lax.broadcasted_iota(jnp.int32, (MCHUNK, MCHUNK), 0)
    s_idx = lax.broadcasted_iota(jnp.int32, (MCHUNK, MCHUNK), 1)
    masks = (s_idx <= t_idx, s_idx >= t_idx)
    ones = jnp.ones((MCHUNK, V7X_LANES), BF16)
    dirs = ((qf_ref, ktf_ref, vf_ref, rf_ref, cf_ref, sf_ref, hf_ref),
            (qb_ref, ktb_ref, vb_ref, rb_ref, cb_ref, sb_ref, hb_ref))

    def chunk_body(step, carry):
        for d, (q_ref, kt_ref, v_ref, r_ref, col_ref, cs_ref, h_ref) in enumerate(dirs):
            c = step if d == 0 else cps - 1 - step
            for hh in range(N_HEADS):
                i = d * N_HEADS + hh
                hs = slice(hh * HEAD_DIM, (hh + 1) * HEAD_DIM)
                q = q_ref[0, c, :, hs]
                kt = kt_ref[0, c, hs, :]
                v = v_ref[0, c, :, hs]
                r_row = r_ref[0, c, i:i + 1, :]
                cmr = jnp.broadcast_to(col_ref[0, c, :, i:i + 1], (MCHUNK, V7X_LANES))
                b_cum = jnp.broadcast_to(col_ref[0, c, :, N_STATES + i:N_STATES + i + 1],
                                         (MCHUNK, V7X_LANES))
                r_max = cs_ref[0, c, i:i + 1, :]
                b_tot = cs_ref[0, c, N_STATES + i:N_STATES + i + 1, :]
                m_row = m_scr[i]

                big_m = jnp.maximum(cmr, m_row)
                d_exp = jnp.exp(jnp.where(masks[d], r_row - _twice(big_m), NEG_BIG))
                p = jnp.dot(q, kt, preferred_element_type=F32) * d_exp
                p_bf = p.astype(BF16)
                q_w = q * _twice(jnp.exp(m_row - big_m).astype(BF16))
                den_raw = (jnp.dot(p_bf, ones, preferred_element_type=F32)
                           + jnp.dot(q_w, nbf_scr[i], preferred_element_type=F32))
                den = jnp.maximum(jnp.abs(den_raw), jnp.exp(-(b_cum + big_m)))
                inv = 1.0 / den
                num = (jnp.dot(p_bf, v, preferred_element_type=F32)
                       + jnp.dot(q_w, cbf_scr[i], preferred_element_type=F32))
                h_ref[0, c, :, hs] = (num * _twice(inv)).astype(BF16)

                m_x = jnp.maximum(m_row, r_max)
                w_row = jnp.exp(r_row - _twice(m_x))
                decay = jnp.exp(m_row - m_x)
                ktw = kt * w_row.astype(BF16)
                c_new = _twice(decay) * c_scr[i] + jnp.dot(ktw, v, preferred_element_type=F32)
                c_scr[i] = c_new
                cbf_scr[i] = c_new.astype(BF16)
                n_new = decay * n_scr[i] + jnp.dot(ktw, ones, preferred_element_type=F32)
                n_scr[i] = n_new
                nbf_scr[i] = n_new.astype(BF16)
                m_scr[i] = b_tot + m_x
        return carry

    lax.fori_loop(0, cps, chunk_body, 0)


def _mlstm(q4, kt4, v4, rrow, ccol, cs):
    bsz, n_chunks = q4.shape[:2]
    cps = min(MLSTM_CHUNKS, n_chunks)
    n_steps = n_chunks // cps

    def specs(index):
        blk = lambda r, c: pl.BlockSpec((1, cps, r, c), lambda b, j: (b, index(j), 0, 0))
        return [blk(MCHUNK, D_MODEL), blk(D_MODEL, MCHUNK), blk(MCHUNK, D_MODEL),
                blk(N_STATES, MCHUNK), blk(MCHUNK, 2 * N_STATES), blk(2 * N_STATES, V7X_LANES)]

    fwd = lambda j: j
    bwd = lambda j: n_steps - 1 - j
    h_shape = jax.ShapeDtypeStruct(q4.shape, BF16)
    return pl.pallas_call(
        _mlstm_kernel,
        grid=(bsz, n_steps),
        in_specs=specs(fwd) + specs(bwd),
        out_specs=[specs(fwd)[0], specs(bwd)[0]],
        out_shape=[h_shape, h_shape],
        scratch_shapes=[
            pltpu.VMEM((N_STATES, HEAD_DIM, HEAD_DIM), F32),
            pltpu.VMEM((N_STATES, HEAD_DIM, HEAD_DIM), BF16),
            pltpu.VMEM((N_STATES, HEAD_DIM, V7X_LANES), F32),
            pltpu.VMEM((N_STATES, HEAD_DIM, V7X_LANES), BF16),
            pltpu.VMEM((N_STATES, 1, V7X_LANES), F32),
        ],
        compiler_params=pltpu.CompilerParams(
            dimension_semantics=("parallel", "arbitrary"), vmem_limit_bytes=52 * 2**20),
        name="mlstm",
    )(q4, kt4, v4, rrow, ccol, cs, q4, kt4, v4, rrow, ccol, cs)


def _out_kernel(hf_ref, hb_ref, yc_ref, h_ref, x_ref, gate_ref, mhg_ref, wb_ref, wpm_ref,
                wout_ref, fg_ref, o_ref, *rest, final):
    ym_scr, merged_scr = rest[-2:]
    ts = x_ref.shape[1]

    def mm(col):
        return jnp.dot(h_ref[0], wb_ref[:, col:col + COL_BLOCK], preferred_element_type=F32)

    for hh in range(N_HEADS):
        hs = slice(hh * HEAD_DIM, (hh + 1) * HEAD_DIM)
        mz = mm(OFF_MZ + hh * HEAD_DIM)
        og = _sigmoid(mm(OFF_O + hh * HEAD_DIM)) * (mz * _sigmoid(mz))
        hm = hf_ref[0, :, hs].astype(F32) + hb_ref[0, :, hs].astype(F32)
        hm = hm * _rms_scale(hm) * mhg_ref[0:1, hs]
        ym_scr[:, hs] = (og * hm).astype(BF16)
    for c0 in range(0, D_MODEL, COL_BLOCK):
        cols = slice(c0, c0 + COL_BLOCK)
        ym = jnp.dot(ym_scr[...], wpm_ref[:, cols], preferred_element_type=F32)
        merged = yc_ref[0, :, cols].astype(F32) + _sigmoid(mm(OFF_GB + c0)) * ym
        merged_scr[:, cols] = merged.astype(BF16)
    for c0 in range(0, D_MODEL, COL_BLOCK):
        cols = slice(c0, c0 + COL_BLOCK)
        y = jnp.dot(merged_scr[...], wout_ref[:, cols], preferred_element_type=F32)
        o_ref[0, :, cols] = x_ref[0, :, cols] + gate_ref[0, :, cols] * y
    if final:
        x_new = o_ref[0]
        o_ref[0] = x_new * _rms_scale(x_new) * fg_ref[...]
    else:
        e_ref = rest[0]
        for t in range(ts // PROJ_TILE):
            r0 = t * PROJ_TILE
            e_ref[0, t, 0:1, :] = o_ref[0, r0:r0 + 1, :]
            e_ref[0, t, 1:2, :] = o_ref[0, r0 + PROJ_TILE - 1:r0 + PROJ_TILE, :]
            e_ref[0, t, 2:EDGE_ROWS, :] = jnp.zeros((EDGE_ROWS - 2, D_MODEL), F32)


def _out(hf, hb, yc, h, x, gate, mh_g, w_b, w_pm, w_out, final_g, final):
    bsz, seq, _ = x.shape
    ts = math.gcd(seq, OUT_TILE)
    n_tiles = seq // PROJ_TILE
    edges_per_step = ts // PROJ_TILE
    tok = pl.BlockSpec((1, ts, D_MODEL), lambda b, i: (b, i, 0))
    per_row = pl.BlockSpec((1, 1, D_MODEL), lambda b, i: (b, 0, 0))

    def const(shape):
        return pl.BlockSpec(shape, lambda b, i: (0,) * len(shape), pipeline_mode=pl.Buffered(1))

    out_specs = [tok]
    out_shape = [jax.ShapeDtypeStruct(x.shape, F32)]
    if not final:
        out_specs.append(pl.BlockSpec((1, edges_per_step, EDGE_ROWS, D_MODEL),
                                      lambda b, i: (b, i, 0, 0)))
        out_shape.append(jax.ShapeDtypeStruct((bsz, n_tiles, EDGE_ROWS, D_MODEL), F32))
    tile_bf16 = pltpu.VMEM((ts, D_MODEL), BF16)
    return pl.pallas_call(
        functools.partial(_out_kernel, final=final),
        grid=(bsz, seq // ts),
        in_specs=[tok, tok, tok, tok, tok, per_row,
                  const((1, D_MODEL)), const((D_MODEL, W_B_COLS)), const((D_MODEL, D_MODEL)),
                  const((D_MODEL, D_MODEL)), const((1, D_MODEL))],
        out_specs=out_specs,
        out_shape=out_shape,
        scratch_shapes=[tile_bf16, tile_bf16],
        compiler_params=pltpu.CompilerParams(
            dimension_semantics=("parallel", "parallel"), vmem_limit_bytes=56 * 2**20),
        name="out",
    )(hf, hb, yc, h, x, gate, mh_g, w_b, w_pm, w_out, final_g)


def kernel(x_prompt, x_sample, c_prompt, c_sample, w_ada, b_ada, norm_g, w_in, b_gates, conv_w,
           conv_b, mh_norm_g, w_proj_conv, w_proj_mlstm, w_out, final_norm_g):
    depth = w_ada.shape[0]
    xs = [x_prompt, x_sample]
    row_slices, row0 = [], 0
    for x in xs:
        row_slices.append(slice(row0, row0 + x.shape[0]))
        row0 += x.shape[0]
    a_all, shift_all, gate_all = _ada(jnp.concatenate([c_prompt, c_sample], axis=0),
                                      w_ada, b_ada, norm_g)
    perm = jnp.array(GATE_PERM)
    final_g = final_norm_g.reshape(1, D_MODEL)
    edges = [_edges(x) for x in xs]
    w_kg_all = _wkg(w_in)

    for l in range(depth):
        w = w_in[l]
        cb_, cc_, cx_, cz_, q_, _, v_, o_, mz_ = (
            w[:, i * D_MODEL:(i + 1) * D_MODEL] for i in range(9))
        w_ga = w[:, 9 * D_MODEL + N_GATES:10 * D_MODEL + N_GATES]
        w_gb = w[:, 10 * D_MODEL + N_GATES:]
        w_a = jnp.concatenate([cb_, cc_, cx_, cz_, q_, v_, w_ga], axis=1).astype(BF16)
        w_b = jnp.concatenate([o_, mz_, w_gb], axis=1).astype(BF16)
        w_pc = w_proj_conv[l].astype(BF16)
        w_pm = w_proj_mlstm[l].astype(BF16)
        w_o = w_out[l].astype(BF16)
        b_g = b_gates[l].reshape(N_GATES)[perm].reshape(N_GATES, 1)
        cbias = conv_b[l].reshape(1, D_MODEL)
        mh_g = mh_norm_g[l].reshape(1, D_MODEL)
        final = l == depth - 1

        new_xs, new_edges = [], []
        for x, edge, rows in zip(xs, edges, row_slices):
            nb = x.shape[0]
            a = a_all[l, rows].reshape(nb, 1, D_MODEL)
            s = shift_all[l, rows].reshape(nb, 1, D_MODEL)
            gate = gate_all[l, rows].reshape(nb, 1, D_MODEL)
            u_edges = _halo(edge, a, s, w_a)
            q4, kt4, v4, yc, h, rrow, ccol, cs = _proj(
                x, u_edges, a, s, w_a, w_kg_all, l, w_pc, conv_w[l], cbias, b_g)
            hf, hb = _mlstm(q4, kt4, v4, rrow, ccol, cs)
            res = _out(hf.reshape(x.shape), hb.reshape(x.shape), yc, h, x, gate,
                       mh_g, w_b, w_pm, w_o, final_g, final=final)
            new_xs.append(res[0])
            new_edges.append(None if final else res[1])
        xs, edges = new_xs, new_edges
    return tuple(xs)
```

```python
import functools
import math
import typing

import jax
import jax.numpy as jnp
from jax import lax
from jax.experimental import pallas as pl
from jax.experimental.pallas import tpu as pltpu

F32 = jnp.float32
BF16 = jnp.bfloat16

D_MODEL = 1024
N_HEADS = 4
HEAD_DIM = D_MODEL // N_HEADS
N_GATES = 16
N_STATES = 2 * N_HEADS
EPS = 1e-6
NEG_BIG = -1e30

V7X_LANES = 128
V7X_SUBLANES = 8
V7X_MXU_WIDTH = 256

OFF_CB, OFF_CC, OFF_CX, OFF_CZ = 0, 1 * D_MODEL, 2 * D_MODEL, 3 * D_MODEL
OFF_Q, OFF_V, OFF_GA = 4 * D_MODEL, 5 * D_MODEL, 6 * D_MODEL
W_A_COLS = 7 * D_MODEL
OFF_O, OFF_MZ, OFF_GB = 0, 1 * D_MODEL, 2 * D_MODEL
W_B_COLS = 3 * D_MODEL

GATE_PERM = (0, 1, 2, 3, 8, 9, 10, 11, 4, 5, 6, 7, 12, 13, 14, 15)

PROJ_TILE = 512
OUT_TILE = 1024
NORM_ROWS = 64
COL_BLOCK = 256
MCHUNK = 256
MLSTM_CHUNKS = 4
EDGE_ROWS = V7X_SUBLANES
HALO_BATCH = 8


class _Modulation(typing.NamedTuple):
    a: jax.Array
    shift: jax.Array
    gate: jax.Array
    base: int


def _sigmoid(x):
    return jax.nn.sigmoid(x)


def _rms_scale(x):
    return lax.rsqrt(jnp.mean(x * x, axis=-1, keepdims=True) + EPS)


def _twice(x):
    return jnp.concatenate([x, x], axis=1)


def _modulated_norm(x_ref, a_ref, s_ref, h_ref):
    for r0 in range(0, x_ref.shape[1], NORM_ROWS):
        rows = slice(r0, r0 + NORM_ROWS)
        x = x_ref[0, rows, :]
        h_ref[0, rows, :] = (x * _rms_scale(x) * a_ref[0] + s_ref[0]).astype(BF16)


def _ada_kernel(c_ref, w_ref, b_ref, g_ref, a_ref, shift_ref, gate_ref):
    mod = jnp.dot(c_ref[...], w_ref[0], preferred_element_type=F32,
                  precision=lax.Precision.HIGHEST) + b_ref[0]
    shift_ref[0] = mod[:, :D_MODEL]
    a_ref[0] = g_ref[0] * (1.0 + mod[:, D_MODEL:2 * D_MODEL])
    gate_ref[0] = mod[:, 2 * D_MODEL:]


def _ada(c_all, w_ada, b_ada, norm_g):
    depth = w_ada.shape[0]
    rows = c_all.shape[0]
    out = jax.ShapeDtypeStruct((depth, rows, D_MODEL), F32)
    vec = pl.BlockSpec((1, rows, D_MODEL), lambda l: (l, 0, 0))
    return pl.pallas_call(
        _ada_kernel,
        grid=(depth,),
        in_specs=[
            pl.BlockSpec((rows, D_MODEL), lambda l: (0, 0)),
            pl.BlockSpec((1, D_MODEL, 3 * D_MODEL), lambda l: (l, 0, 0)),
            pl.BlockSpec((1, 1, 3 * D_MODEL), lambda l: (l, 0, 0)),
            pl.BlockSpec((1, 1, D_MODEL), lambda l: (l, 0, 0)),
        ],
        out_specs=[vec, vec, vec],
        out_shape=[out, out, out],
        compiler_params=pltpu.CompilerParams(
            dimension_semantics=("arbitrary",), vmem_limit_bytes=40 * 2**20),
        name="ada",
    )(c_all, w_ada, b_ada.reshape(depth, 1, 3 * D_MODEL), norm_g.reshape(depth, 1, D_MODEL))


def _wkg_kernel(wk_ref, wg_ref, o_ref):
    o_ref[0, :D_MODEL, :] = (wk_ref[0] * (HEAD_DIM ** -0.5)).T.astype(BF16)
    gates_t = wg_ref[0].T
    o_ref[0, D_MODEL:, :] = jnp.concatenate(
        [gates_t[g:g + 1] for g in GATE_PERM], axis=0).astype(BF16)


def _wkg(w_in):
    depth = w_in.shape[0]
    w_k = w_in[:, :, 5 * D_MODEL:6 * D_MODEL]
    w_gates = w_in[:, :, 9 * D_MODEL:9 * D_MODEL + V7X_LANES]
    return pl.pallas_call(
        _wkg_kernel,
        grid=(depth,),
        in_specs=[pl.BlockSpec((1, D_MODEL, D_MODEL), lambda l: (l, 0, 0)),
                  pl.BlockSpec((1, D_MODEL, V7X_LANES), lambda l: (l, 0, 0))],
        out_specs=pl.BlockSpec((1, D_MODEL + N_GATES, D_MODEL), lambda l: (l, 0, 0)),
        out_shape=jax.ShapeDtypeStruct((depth, D_MODEL + N_GATES, D_MODEL), BF16),
        compiler_params=pltpu.CompilerParams(dimension_semantics=("parallel",)),
        name="wkg",
    )(w_k, w_gates)


def _edges_kernel(first_ref, last_ref, e_ref):
    n_tiles = e_ref.shape[1]
    e_ref[0, :, 0:1, :] = first_ref[0, :, 0:1, :]
    e_ref[0, :, 1:2, :] = last_ref[0, :, V7X_SUBLANES - 1:V7X_SUBLANES, :]
    e_ref[0, :, 2:EDGE_ROWS, :] = jnp.zeros((n_tiles, EDGE_ROWS - 2, D_MODEL), F32)


def _edges(x):
    bsz, seq, _ = x.shape
    n_tiles = seq // PROJ_TILE
    groups = PROJ_TILE // V7X_SUBLANES
    x4 = x.reshape(bsz, n_tiles, PROJ_TILE, D_MODEL)
    group = lambda g: pl.BlockSpec((1, n_tiles, V7X_SUBLANES, D_MODEL),
                                   lambda b: (b, 0, g, 0))
    return pl.pallas_call(
        _edges_kernel,
        grid=(bsz,),
        in_specs=[group(0), group(groups - 1)],
        out_specs=pl.BlockSpec((1, n_tiles, EDGE_ROWS, D_MODEL), lambda b: (b, 0, 0, 0)),
        out_shape=jax.ShapeDtypeStruct((bsz, n_tiles, EDGE_ROWS, D_MODEL), F32),
        compiler_params=pltpu.CompilerParams(dimension_semantics=("parallel",)),
        name="edges",
    )(x4, x4)


def _halo_kernel(x_ref, a_ref, s_ref, wcc_ref, wcx_ref, u_ref):
    nb, rows, _ = x_ref.shape
    x = x_ref[...]
    h = (x * _rms_scale(x) * a_ref[...] + s_ref[...]).astype(BF16).reshape(nb * rows, D_MODEL)
    cc = jnp.dot(h, wcc_ref[0], preferred_element_type=F32)
    cx = jnp.dot(h, wcx_ref[0], preferred_element_type=F32)
    u_ref[...] = (cc * cx).reshape(nb, rows, D_MODEL)


def _halo(edges, mod, w_a, layer):
    bsz, n_tiles = edges.shape[:2]
    rows = n_tiles * EDGE_ROWS
    nb = math.gcd(math.gcd(bsz, HALO_BATCH), mod.base)
    blk = pl.BlockSpec((nb, rows, D_MODEL), lambda b: (b, 0, 0))
    per_row = pl.BlockSpec((nb, 1, D_MODEL), lambda b: (mod.base // nb + b, 0, 0))
    col_block = lambda off: pl.BlockSpec((1, D_MODEL, D_MODEL),
                                         lambda b: (layer, 0, off // D_MODEL))
    u = pl.pallas_call(
        _halo_kernel,
        grid=(bsz // nb,),
        in_specs=[blk, per_row, per_row, col_block(OFF_CC), col_block(OFF_CX)],
        out_specs=blk,
        out_shape=jax.ShapeDtypeStruct((bsz, rows, D_MODEL), F32),
        compiler_params=pltpu.CompilerParams(dimension_semantics=("parallel",)),
        name="halo",
    )(edges.reshape(bsz, rows, D_MODEL), mod.a, mod.shift, w_a, w_a)
    return u.reshape(edges.shape)


def _lane_scan(x, lane, op, fill, suffix):
    width = x.shape[1]
    s = 1
    while s < width:
        if suffix:
            shifted = jnp.where(lane < width - s, pltpu.roll(x, width - s, 1), fill)
        else:
            shifted = jnp.where(lane >= s, pltpu.roll(x, s, 1), fill)
        x = op(x, shifted)
        s *= 2
    return x


def _proj_kernel(x_ref, up_ref, un_ref, a_ref, s_ref, wa_ref, wkg_ref, wpc_ref, cw_ref, cbias_ref,
                 bg_ref, q_ref, kt_ref, v_ref, yc_ref, h_ref, rrow_ref, ccol_ref, cs_ref,
                 yconv_scr):
    ts = x_ref.shape[1]
    n_chunks = ts // MCHUNK
    tile = pl.program_id(1)
    _modulated_norm(x_ref, a_ref, s_ref, h_ref)

    def mm(col, width=COL_BLOCK):
        return jnp.dot(h_ref[0], wa_ref[0, :, col:col + width], preferred_element_type=F32)

    def qv_block(c0):
        cols = slice(c0, c0 + COL_BLOCK)
        q_ref[0, :, :, cols] = mm(OFF_Q + c0).astype(BF16).reshape(n_chunks, MCHUNK, COL_BLOCK)
        v_ref[0, :, :, cols] = mm(OFF_V + c0).astype(BF16).reshape(n_chunks, MCHUNK, COL_BLOCK)

    qv_block(0)

    lane = lax.broadcasted_iota(jnp.int32, (N_STATES, MCHUNK), 1)
    is_fwd = lax.broadcasted_iota(jnp.int32, (N_STATES, MCHUNK), 0) < N_HEADS
    pad_rows = jnp.zeros((V7X_LANES - 2 * N_STATES, V7X_LANES), F32)
    for c in range(n_chunks):
        kg = lax.dot_general(wkg_ref[0], h_ref[0, c * MCHUNK:(c + 1) * MCHUNK, :],
                             (((1,), (1,)), ((), ())),
                             preferred_element_type=F32)
        kt_ref[0, c] = kg[:D_MODEL].astype(BF16)

        gates = kg[D_MODEL:] + bg_ref[0]
        lf = jax.nn.log_sigmoid(gates[N_STATES:])
        b = jnp.where(is_fwd, _lane_scan(lf, lane, jnp.add, 0.0, False),
                      _lane_scan(lf, lane, jnp.add, 0.0, True))
        r = gates[:N_STATES] - b
        cmr = jnp.where(is_fwd, _lane_scan(r, lane, jnp.maximum, NEG_BIG, False),
                        _lane_scan(r, lane, jnp.maximum, NEG_BIG, True))
        rrow_ref[0, c] = r
        cs_ref[0, c, 0:N_STATES, :] = jnp.broadcast_to(
            jnp.max(r, axis=1, keepdims=True), (N_STATES, V7X_LANES))
        cs_ref[0, c, N_STATES:, :] = jnp.broadcast_to(
            jnp.sum(lf, axis=1, keepdims=True), (N_STATES, V7X_LANES))
        for half in range(MCHUNK // V7X_LANES):
            ls = slice(half * V7X_LANES, (half + 1) * V7X_LANES)
            cols_t = jnp.concatenate([cmr[:, ls], b[:, ls], pad_rows], axis=0).T
            ccol_ref[0, c, ls, :] = cols_t[:, :2 * N_STATES]

    row = lax.broadcasted_iota(jnp.int32, (ts, COL_BLOCK), 0)
    has_prev = tile > 0
    has_next = tile < pl.num_programs(1) - 1
    for c0 in range(0, D_MODEL, COL_BLOCK):
        cols = slice(c0, c0 + COL_BLOCK)
        u = mm(OFF_CC + c0) * mm(OFF_CX + c0)
        edge_prev = jnp.where(has_prev, up_ref[0, 0, 1:2, cols], 0.0)
        edge_next = jnp.where(has_next, un_ref[0, 0, 0:1, cols], 0.0)
        u_prev = jnp.where(row == 0, edge_prev, pltpu.roll(u, 1, 0))
        u_next = jnp.where(row == ts - 1, edge_next, pltpu.roll(u, ts - 1, 0))
        conv = (u_prev * cw_ref[0, 0:1, cols] + u * cw_ref[0, 1:2, cols]
                + u_next * cw_ref[0, 2:3, cols] + cbias_ref[0, 0:1, cols])
        cz = mm(OFF_CZ + c0)
        y = mm(OFF_CB + c0) * conv * (cz * _sigmoid(cz))
        yconv_scr[:, cols] = y.astype(BF16)

    for c0 in range(0, D_MODEL, COL_BLOCK):
        cols = slice(c0, c0 + COL_BLOCK)
        yc = jnp.dot(yconv_scr[...], wpc_ref[0, :, cols], preferred_element_type=F32)
        yc_ref[0, :, cols] = (_sigmoid(mm(OFF_GA + c0)) * yc).astype(BF16)
        if c0 > 0:
            qv_block(c0)


def _layer_block(layer, *shape):
    return pl.BlockSpec((1,) + shape, lambda b, i: (layer,) + (0,) * len(shape),
                        pipeline_mode=pl.Buffered(1))


def _proj(x, u_edges, mod, layer, w_a, w_kg, w_pc, conv_w, conv_b, b_g):
    bsz, seq, _ = x.shape
    ts = PROJ_TILE
    n_tiles = seq // ts
    n_chunks = seq // MCHUNK
    cps = ts // MCHUNK
    const = functools.partial(_layer_block, layer)
    tok = pl.BlockSpec((1, ts, D_MODEL), lambda b, i: (b, i, 0))
    per_row = pl.BlockSpec((1, 1, D_MODEL), lambda b, i: (mod.base + b, 0, 0))
    edge = lambda index: pl.BlockSpec((1, 1, EDGE_ROWS, D_MODEL),
                                      lambda b, i: (b, index(i), 0, 0))
    chunked = lambda r, c: pl.BlockSpec((1, cps, r, c), lambda b, i: (b, i, 0, 0))
    return pl.pallas_call(
        _proj_kernel,
        grid=(bsz, n_tiles),
        in_specs=[
            tok,
            edge(lambda i: jnp.maximum(i - 1, 0)), edge(lambda i: jnp.minimum(i + 1, n_tiles - 1)),
            per_row, per_row,
            const(D_MODEL, W_A_COLS), const(D_MODEL + N_GATES, D_MODEL),
            const(D_MODEL, D_MODEL), const(3, D_MODEL), const(1, D_MODEL), const(N_GATES, 1),
        ],
        out_specs=[
            chunked(MCHUNK, D_MODEL), chunked(D_MODEL, MCHUNK), chunked(MCHUNK, D_MODEL),
            tok, tok,
            chunked(N_STATES, MCHUNK), chunked(MCHUNK, 2 * N_STATES),
            chunked(2 * N_STATES, V7X_LANES),
        ],
        out_shape=[
            jax.ShapeDtypeStruct((bsz, n_chunks, MCHUNK, D_MODEL), BF16),
            jax.ShapeDtypeStruct((bsz, n_chunks, D_MODEL, MCHUNK), BF16),
            jax.ShapeDtypeStruct((bsz, n_chunks, MCHUNK, D_MODEL), BF16),
            jax.ShapeDtypeStruct((bsz, seq, D_MODEL), BF16),
            jax.ShapeDtypeStruct((bsz, seq, D_MODEL), BF16),
            jax.ShapeDtypeStruct((bsz, n_chunks, N_STATES, MCHUNK), F32),
            jax.ShapeDtypeStruct((bsz, n_chunks, MCHUNK, 2 * N_STATES), F32),
            jax.ShapeDtypeStruct((bsz, n_chunks, 2 * N_STATES, V7X_LANES), F32),
        ],
        scratch_shapes=[pltpu.VMEM((ts, D_MODEL), BF16)],
        compiler_params=pltpu.CompilerParams(
            dimension_semantics=("parallel", "parallel"), vmem_limit_bytes=52 * 2**20),
        name="proj",
    )(x, u_edges, u_edges, mod.a, mod.shift, w_a, w_kg, w_pc, conv_w, conv_b, b_g)


def _mlstm_kernel(qf_ref, ktf_ref, vf_ref, rf_ref, cf_ref, sf_ref,
                  qb_ref, ktb_ref, vb_ref, rb_ref, cb_ref, sb_ref,
                  hf_ref, hb_ref,
                  c_scr, cbf_scr, n_scr, nbf_scr, m_scr):
    cps = qf_ref.shape[1]

    @pl.when(pl.program_id(1) == 0)
    def _():
        c_scr[...] = jnp.zeros_like(c_scr)
        cbf_scr[...] = jnp.zeros_like(cbf_scr)
        n_scr[...] = jnp.zeros_like(n_scr)
        nbf_scr[...] = jnp.zeros_like(nbf_scr)
        m_scr[...] = jnp.zeros_like(m_scr)

    t_idx = lax.broadcasted_iota(jnp.int32, (MCHUNK, MCHUNK), 0)
    s_idx = lax.broadcasted_iota(jnp.int32, (MCHUNK, MCHUNK), 1)
    masks = (s_idx <= t_idx, s_idx >= t_idx)
    ones = jnp.ones((MCHUNK, V7X_LANES), BF16)
    dirs = ((qf_ref, ktf_ref, vf_ref, rf_ref, cf_ref, sf_ref, hf_ref),
            (qb_ref, ktb_ref, vb_ref, rb_ref, cb_ref, sb_ref, hb_ref))

    def chunk_body(step, carry):
        for d, (q_ref, kt_ref, v_ref, r_ref, col_ref, cs_ref, h_ref) in enumerate(dirs):
            c = step if d == 0 else cps - 1 - step
            for hh in range(N_HEADS):
                i = d * N_HEADS + hh
                hs = slice(hh * HEAD_DIM, (hh + 1) * HEAD_DIM)
                q = q_ref[0, c, :, hs]
                kt = kt_ref[0, c, hs, :]
                v = v_ref[0, c, :, hs]
                r_row = r_ref[0, c, i:i + 1, :]
                cmr = jnp.broadcast_to(col_ref[0, c, :, i:i + 1], (MCHUNK, V7X_LANES))
                b_cum = jnp.broadcast_to(col_ref[0, c, :, N_STATES + i:N_STATES + i + 1],
                                         (MCHUNK, V7X_LANES))
                r_max = cs_ref[0, c, i:i + 1, :]
                b_tot = cs_ref[0, c, N_STATES + i:N_STATES + i + 1, :]
                m_row = m_scr[i]

                big_m = jnp.maximum(cmr, m_row)
                d_exp = jnp.exp(jnp.where(masks[d], r_row - _twice(big_m), NEG_BIG))
                p = jnp.dot(q, kt, preferred_element_type=F32) * d_exp
                p_bf = p.astype(BF16)
                inter_w = jnp.exp(m_row - big_m)
                q_n = jnp.dot(q, nbf_scr[i], preferred_element_type=F32)
                p_sum = jnp.dot(p_bf, ones, preferred_element_type=F32)
                den = jnp.maximum(jnp.abs(p_sum + inter_w * q_n), jnp.exp(-(b_cum + big_m)))
                inv = 1.0 / den
                q_c = jnp.dot(q, cbf_scr[i], preferred_element_type=F32)
                num = jnp.dot(p_bf, v, preferred_element_type=F32) + _twice(inter_w) * q_c
                h_ref[0, c, :, hs] = (num * _twice(inv)).astype(BF16)

                m_x = jnp.maximum(m_row, r_max)
                w_row = jnp.exp(r_row - _twice(m_x))
                decay = jnp.exp(m_row - m_x)
                ktw = kt * w_row.astype(BF16)
                c_new = _twice(decay) * c_scr[i] + jnp.dot(ktw, v, preferred_element_type=F32)
                c_scr[i] = c_new
                cbf_scr[i] = c_new.astype(BF16)
                n_new = decay * n_scr[i] + jnp.dot(ktw, ones, preferred_element_type=F32)
                n_scr[i] = n_new
                nbf_scr[i] = n_new.astype(BF16)
                m_scr[i] = b_tot + m_x
        return carry

    lax.fori_loop(0, cps, chunk_body, 0)


def _mlstm(q4, kt4, v4, rrow, ccol, cs):
    bsz, n_chunks = q4.shape[:2]
    cps = min(MLSTM_CHUNKS, n_chunks)
    n_steps = n_chunks // cps

    def specs(index):
        blk = lambda r, c: pl.BlockSpec((1, cps, r, c), lambda b, j: (b, index(j), 0, 0))
        return [blk(MCHUNK, D_MODEL), blk(D_MODEL, MCHUNK), blk(MCHUNK, D_MODEL),
                blk(N_STATES, MCHUNK), blk(MCHUNK, 2 * N_STATES), blk(2 * N_STATES, V7X_LANES)]

    fwd = lambda j: j
    bwd = lambda j: n_steps - 1 - j
    h_shape = jax.ShapeDtypeStruct(q4.shape, BF16)
    return pl.pallas_call(
        _mlstm_kernel,
        grid=(bsz, n_steps),
        in_specs=specs(fwd) + specs(bwd),
        out_specs=[specs(fwd)[0], specs(bwd)[0]],
        out_shape=[h_shape, h_shape],
        scratch_shapes=[
            pltpu.VMEM((N_STATES, HEAD_DIM, HEAD_DIM), F32),
            pltpu.VMEM((N_STATES, HEAD_DIM, HEAD_DIM), BF16),
            pltpu.VMEM((N_STATES, HEAD_DIM, V7X_LANES), F32),
            pltpu.VMEM((N_STATES, HEAD_DIM, V7X_LANES), BF16),
            pltpu.VMEM((N_STATES, 1, V7X_LANES), F32),
        ],
        compiler_params=pltpu.CompilerParams(
            dimension_semantics=("parallel", "arbitrary"), vmem_limit_bytes=52 * 2**20),
        name="mlstm",
    )(q4, kt4, v4, rrow, ccol, cs, q4, kt4, v4, rrow, ccol, cs)


def _out_kernel(hf_ref, hb_ref, yc_ref, h_ref, x_ref, gate_ref, mhg_ref, wb_ref, wpm_ref,
                wout_ref, fg_ref, o_ref, *rest, final):
    ym_scr, merged_scr = rest[-2:]
    ts = x_ref.shape[1]

    def mm(col):
        return jnp.dot(h_ref[0], wb_ref[0, :, col:col + COL_BLOCK], preferred_element_type=F32)

    for hh in range(N_HEADS):
        hs = slice(hh * HEAD_DIM, (hh + 1) * HEAD_DIM)
        mz = mm(OFF_MZ + hh * HEAD_DIM)
        og = _sigmoid(mm(OFF_O + hh * HEAD_DIM)) * (mz * _sigmoid(mz))
        hm = hf_ref[0, :, hs].astype(F32) + hb_ref[0, :, hs].astype(F32)
        hm = hm * _rms_scale(hm) * mhg_ref[0, 0:1, hs]
        ym_scr[:, hs] = (og * hm).astype(BF16)
    for c0 in range(0, D_MODEL, COL_BLOCK):
        cols = slice(c0, c0 + COL_BLOCK)
        ym = jnp.dot(ym_scr[...], wpm_ref[0, :, cols], preferred_element_type=F32)
        merged = yc_ref[0, :, cols].astype(F32) + _sigmoid(mm(OFF_GB + c0)) * ym
        merged_scr[:, cols] = merged.astype(BF16)
    for c0 in range(0, D_MODEL, COL_BLOCK):
        cols = slice(c0, c0 + COL_BLOCK)
        y = jnp.dot(merged_scr[...], wout_ref[0, :, cols], preferred_element_type=F32)
        o_ref[0, :, cols] = x_ref[0, :, cols] + gate_ref[0, :, cols] * y
    if final:
        x_new = o_ref[0]
        o_ref[0] = x_new * _rms_scale(x_new) * fg_ref[...]
    else:
        e_ref = rest[0]
        for t in range(ts // PROJ_TILE):
            r0 = t * PROJ_TILE
            e_ref[0, t, 0:1, :] = o_ref[0, r0:r0 + 1, :]
            e_ref[0, t, 1:2, :] = o_ref[0, r0 + PROJ_TILE - 1:r0 + PROJ_TILE, :]
            e_ref[0, t, 2:EDGE_ROWS, :] = jnp.zeros((EDGE_ROWS - 2, D_MODEL), F32)


def _out(hf, hb, yc, h, x, mod, layer, mh_g, w_b, w_pm, w_out, final_g, final):
    bsz, seq, _ = x.shape
    ts = math.gcd(seq, OUT_TILE)
    n_tiles = seq // PROJ_TILE
    edges_per_step = ts // PROJ_TILE
    const = functools.partial(_layer_block, layer)
    tok = pl.BlockSpec((1, ts, D_MODEL), lambda b, i: (b, i, 0))
    per_row = pl.BlockSpec((1, 1, D_MODEL), lambda b, i: (mod.base + b, 0, 0))
    out_specs = [tok]
    out_shape = [jax.ShapeDtypeStruct(x.shape, F32)]
    if not final:
        out_specs.append(pl.BlockSpec((1, edges_per_step, EDGE_ROWS, D_MODEL),
                                      lambda b, i: (b, i, 0, 0)))
        out_shape.append(jax.ShapeDtypeStruct((bsz, n_tiles, EDGE_ROWS, D_MODEL), F32))
    tile_bf16 = pltpu.VMEM((ts, D_MODEL), BF16)
    return pl.pallas_call(
        functools.partial(_out_kernel, final=final),
        grid=(bsz, seq // ts),
        in_specs=[tok, tok, tok, tok, tok, per_row,
                  const(1, D_MODEL), const(D_MODEL, W_B_COLS), const(D_MODEL, D_MODEL),
                  const(D_MODEL, D_MODEL),
                  pl.BlockSpec((1, D_MODEL), lambda b, i: (0, 0))],
        out_specs=out_specs,
        out_shape=out_shape,
        scratch_shapes=[tile_bf16, tile_bf16],
        compiler_params=pltpu.CompilerParams(
            dimension_semantics=("parallel", "parallel"), vmem_limit_bytes=56 * 2**20),
        name="out",
    )(hf, hb, yc, h, x, mod.gate, mh_g, w_b, w_pm, w_out, final_g)


def kernel(x_prompt, x_sample, c_prompt, c_sample, w_ada, b_ada, norm_g, w_in, b_gates, conv_w,
           conv_b, mh_norm_g, w_proj_conv, w_proj_mlstm, w_out, final_norm_g):
    depth = w_ada.shape[0]
    xs = [x_prompt, x_sample]
    n_rows = sum(x.shape[0] for x in xs)
    mod_rows = _ada(jnp.concatenate([c_prompt, c_sample], axis=0), w_ada, b_ada, norm_g)
    a_rows, shift_rows, gate_rows = (m.reshape(depth * n_rows, 1, D_MODEL) for m in mod_rows)

    col = lambda i: w_in[:, :, i * D_MODEL:(i + 1) * D_MODEL]
    gm0 = 9 * D_MODEL + N_GATES
    w_ga, w_gb = w_in[:, :, gm0:gm0 + D_MODEL], w_in[:, :, gm0 + D_MODEL:]
    w_a = jnp.concatenate([col(0), col(1), col(2), col(3), col(4), col(6), w_ga],
                          axis=2).astype(BF16)
    w_b = jnp.concatenate([col(7), col(8), w_gb], axis=2).astype(BF16)
    w_kg = _wkg(w_in)
    w_pc, w_pm, w_o = (w.astype(BF16) for w in (w_proj_conv, w_proj_mlstm, w_out))
    b_g = b_gates.reshape(depth, N_GATES)[:, jnp.array(GATE_PERM)].reshape(depth, N_GATES, 1)
    cbias = conv_b.reshape(depth, 1, D_MODEL)
    mh_g = mh_norm_g.reshape(depth, 1, D_MODEL)
    final_g = final_norm_g.reshape(1, D_MODEL)
    edges = [_edges(x) for x in xs]

    for l in range(depth):
        final = l == depth - 1
        new_xs, new_edges, row0 = [], [], 0
        for x, edge in zip(xs, edges):
            mod = _Modulation(a_rows, shift_rows, gate_rows, l * n_rows + row0)
            row0 += x.shape[0]
            u_edges = _halo(edge, mod, w_a, l)
            q4, kt4, v4, yc, h, rrow, ccol, cs = _proj(
                x, u_edges, mod, l, w_a, w_kg, w_pc, conv_w, cbias, b_g)
            hf, hb = _mlstm(q4, kt4, v4, rrow, ccol, cs)
            res = _out(hf.reshape(x.shape), hb.reshape(x.shape), yc, h, x, mod, l,
                       mh_g, w_b, w_pm, w_o, final_g, final=final)
            new_xs.append(res[0])
            new_edges.append(None if final else res[1])
        xs, edges = new_xs, new_edges
    return tuple(xs)
```

```python
import functools
import math
import typing

import jax
import jax.numpy as jnp
from jax import lax
from jax.experimental import pallas as pl
from jax.experimental.pallas import tpu as pltpu

F32 = jnp.float32
BF16 = jnp.bfloat16

D_MODEL = 1024
N_HEADS = 4
HEAD_DIM = D_MODEL // N_HEADS
N_GATES = 16
N_STATES = 2 * N_HEADS
EPS = 1e-6
NEG_BIG = -1e30

V7X_LANES = 128
V7X_SUBLANES = 8
V7X_MXU_WIDTH = 256

OFF_CB, OFF_CC, OFF_CX, OFF_CZ = 0, 1 * D_MODEL, 2 * D_MODEL, 3 * D_MODEL
OFF_Q, OFF_V, OFF_GA = 4 * D_MODEL, 5 * D_MODEL, 6 * D_MODEL
W_A_COLS = 7 * D_MODEL
OFF_O, OFF_MZ, OFF_GB = 0, 1 * D_MODEL, 2 * D_MODEL
W_B_COLS = 3 * D_MODEL

GATE_PERM = (0, 1, 2, 3, 8, 9, 10, 11, 4, 5, 6, 7, 12, 13, 14, 15)

TOKEN_TILE = 1024
NORM_ROWS = 64
COL_BLOCK = 256
MCHUNK = 256
MLSTM_CHUNKS = 4
EDGE_ROWS = V7X_SUBLANES
HALO_BATCH = 8


class _Modulation(typing.NamedTuple):
    a: jax.Array
    shift: jax.Array
    gate: jax.Array
    base: int


def _sigmoid(x):
    return jax.nn.sigmoid(x)


def _rms_scale(x):
    return lax.rsqrt(jnp.mean(x * x, axis=-1, keepdims=True) + EPS)


def _token_tile(seq):
    return math.gcd(seq, TOKEN_TILE)


def _twice(x):
    return jnp.concatenate([x, x], axis=1)


def _modulated_norm(x_ref, a_ref, s_ref, h_ref):
    for r0 in range(0, x_ref.shape[1], NORM_ROWS):
        rows = slice(r0, r0 + NORM_ROWS)
        x = x_ref[0, rows, :]
        h_ref[0, rows, :] = (x * _rms_scale(x) * a_ref[0] + s_ref[0]).astype(BF16)


def _ada_kernel(c_ref, w_ref, b_ref, g_ref, a_ref, shift_ref, gate_ref):
    mod = jnp.dot(c_ref[...], w_ref[0], preferred_element_type=F32,
                  precision=lax.Precision.HIGHEST) + b_ref[0]
    shift_ref[0] = mod[:, :D_MODEL]
    a_ref[0] = g_ref[0] * (1.0 + mod[:, D_MODEL:2 * D_MODEL])
    gate_ref[0] = mod[:, 2 * D_MODEL:]


def _ada(c_all, w_ada, b_ada, norm_g):
    depth = w_ada.shape[0]
    rows = c_all.shape[0]
    out = jax.ShapeDtypeStruct((depth, rows, D_MODEL), F32)
    vec = pl.BlockSpec((1, rows, D_MODEL), lambda l: (l, 0, 0))
    return pl.pallas_call(
        _ada_kernel,
        grid=(depth,),
        in_specs=[
            pl.BlockSpec((rows, D_MODEL), lambda l: (0, 0)),
            pl.BlockSpec((1, D_MODEL, 3 * D_MODEL), lambda l: (l, 0, 0)),
            pl.BlockSpec((1, 1, 3 * D_MODEL), lambda l: (l, 0, 0)),
            pl.BlockSpec((1, 1, D_MODEL), lambda l: (l, 0, 0)),
        ],
        out_specs=[vec, vec, vec],
        out_shape=[out, out, out],
        compiler_params=pltpu.CompilerParams(
            dimension_semantics=("arbitrary",), vmem_limit_bytes=40 * 2**20),
        name="ada",
    )(c_all, w_ada, b_ada.reshape(depth, 1, 3 * D_MODEL), norm_g.reshape(depth, 1, D_MODEL))


def _wkg_kernel(wk_ref, wg_ref, o_ref):
    o_ref[0, :D_MODEL, :] = (wk_ref[0] * (HEAD_DIM ** -0.5)).T.astype(BF16)
    gates_t = wg_ref[0].T
    o_ref[0, D_MODEL:, :] = jnp.concatenate(
        [gates_t[g:g + 1] for g in GATE_PERM], axis=0).astype(BF16)


def _wkg(w_in):
    depth = w_in.shape[0]
    w_k = w_in[:, :, 5 * D_MODEL:6 * D_MODEL]
    w_gates = w_in[:, :, 9 * D_MODEL:9 * D_MODEL + V7X_LANES]
    return pl.pallas_call(
        _wkg_kernel,
        grid=(depth,),
        in_specs=[pl.BlockSpec((1, D_MODEL, D_MODEL), lambda l: (l, 0, 0)),
                  pl.BlockSpec((1, D_MODEL, V7X_LANES), lambda l: (l, 0, 0))],
        out_specs=pl.BlockSpec((1, D_MODEL + N_GATES, D_MODEL), lambda l: (l, 0, 0)),
        out_shape=jax.ShapeDtypeStruct((depth, D_MODEL + N_GATES, D_MODEL), BF16),
        compiler_params=pltpu.CompilerParams(dimension_semantics=("parallel",)),
        name="wkg",
    )(w_k, w_gates)


def _edges_kernel(first_ref, last_ref, e_ref):
    n_tiles = e_ref.shape[1]
    e_ref[0, :, 0:1, :] = first_ref[0, :, 0:1, :]
    e_ref[0, :, 1:2, :] = last_ref[0, :, V7X_SUBLANES - 1:V7X_SUBLANES, :]
    e_ref[0, :, 2:EDGE_ROWS, :] = jnp.zeros((n_tiles, EDGE_ROWS - 2, D_MODEL), F32)


def _edges(x):
    bsz, seq, _ = x.shape
    tile = _token_tile(seq)
    n_tiles = seq // tile
    groups = tile // V7X_SUBLANES
    x4 = x.reshape(bsz, n_tiles, tile, D_MODEL)
    group = lambda g: pl.BlockSpec((1, n_tiles, V7X_SUBLANES, D_MODEL),
                                   lambda b: (b, 0, g, 0))
    return pl.pallas_call(
        _edges_kernel,
        grid=(bsz,),
        in_specs=[group(0), group(groups - 1)],
        out_specs=pl.BlockSpec((1, n_tiles, EDGE_ROWS, D_MODEL), lambda b: (b, 0, 0, 0)),
        out_shape=jax.ShapeDtypeStruct((bsz, n_tiles, EDGE_ROWS, D_MODEL), F32),
        compiler_params=pltpu.CompilerParams(dimension_semantics=("parallel",)),
        name="edges",
    )(x4, x4)


def _halo_kernel(x_ref, a_ref, s_ref, wcc_ref, wcx_ref, u_ref):
    nb, rows, _ = x_ref.shape
    x = x_ref[...]
    h = (x * _rms_scale(x) * a_ref[...] + s_ref[...]).astype(BF16).reshape(nb * rows, D_MODEL)
    cc = jnp.dot(h, wcc_ref[0], preferred_element_type=F32)
    cx = jnp.dot(h, wcx_ref[0], preferred_element_type=F32)
    u_ref[...] = (cc * cx).reshape(nb, rows, D_MODEL)


def _halo(edges, mod, w_a, layer):
    bsz, n_tiles = edges.shape[:2]
    rows = n_tiles * EDGE_ROWS
    nb = math.gcd(math.gcd(bsz, HALO_BATCH), mod.base)
    blk = pl.BlockSpec((nb, rows, D_MODEL), lambda b: (b, 0, 0))
    per_row = pl.BlockSpec((nb, 1, D_MODEL), lambda b: (mod.base // nb + b, 0, 0))
    col_block = lambda off: pl.BlockSpec((1, D_MODEL, D_MODEL),
                                         lambda b: (0, 0, off // D_MODEL))
    u = pl.pallas_call(
        _halo_kernel,
        grid=(bsz // nb,),
        in_specs=[blk, per_row, per_row, col_block(OFF_CC), col_block(OFF_CX)],
        out_specs=blk,
        out_shape=jax.ShapeDtypeStruct((bsz, rows, D_MODEL), F32),
        compiler_params=pltpu.CompilerParams(dimension_semantics=("parallel",)),
        name="halo",
    )(edges.reshape(bsz, rows, D_MODEL), mod.a, mod.shift, w_a, w_a)
    return u.reshape(edges.shape)


def _lane_scan(x, lane, op, fill, suffix):
    width = x.shape[1]
    s = 1
    while s < width:
        if suffix:
            shifted = jnp.where(lane < width - s, pltpu.roll(x, width - s, 1), fill)
        else:
            shifted = jnp.where(lane >= s, pltpu.roll(x, s, 1), fill)
        x = op(x, shifted)
        s *= 2
    return x


def _proj_kernel(x_ref, up_ref, un_ref, a_ref, s_ref, wa_ref, wkg_ref, wpc_ref, cw_ref, cbias_ref,
                 bg_ref, q_ref, kt_ref, v_ref, yc_ref, h_ref, rrow_ref, ccol_ref, cs_ref,
                 yconv_scr):
    ts = x_ref.shape[1]
    n_chunks = ts // MCHUNK
    tile = pl.program_id(1)
    _modulated_norm(x_ref, a_ref, s_ref, h_ref)

    def mm(col, width=COL_BLOCK):
        return jnp.dot(h_ref[0], wa_ref[0, :, col:col + width], preferred_element_type=F32)

    def qv_block(c0):
        cols = slice(c0, c0 + COL_BLOCK)
        q_ref[0, :, :, cols] = mm(OFF_Q + c0).astype(BF16).reshape(n_chunks, MCHUNK, COL_BLOCK)
        v_ref[0, :, :, cols] = mm(OFF_V + c0).astype(BF16).reshape(n_chunks, MCHUNK, COL_BLOCK)

    qv_block(0)

    lane = lax.broadcasted_iota(jnp.int32, (N_STATES, MCHUNK), 1)
    is_fwd = lax.broadcasted_iota(jnp.int32, (N_STATES, MCHUNK), 0) < N_HEADS
    pad_rows = jnp.zeros((V7X_LANES - 2 * N_STATES, V7X_LANES), F32)
    for c in range(n_chunks):
        kg = lax.dot_general(wkg_ref[0], h_ref[0, c * MCHUNK:(c + 1) * MCHUNK, :],
                             (((1,), (1,)), ((), ())),
                             preferred_element_type=F32)
        kt_ref[0, c] = kg[:D_MODEL].astype(BF16)

        gates = kg[D_MODEL:] + bg_ref[0]
        lf = jax.nn.log_sigmoid(gates[N_STATES:])
        b = jnp.where(is_fwd, _lane_scan(lf, lane, jnp.add, 0.0, False),
                      _lane_scan(lf, lane, jnp.add, 0.0, True))
        r = gates[:N_STATES] - b
        cmr = jnp.where(is_fwd, _lane_scan(r, lane, jnp.maximum, NEG_BIG, False),
                        _lane_scan(r, lane, jnp.maximum, NEG_BIG, True))
        rrow_ref[0, c] = r
        cs_ref[0, c, 0:N_STATES, :] = jnp.broadcast_to(
            jnp.max(r, axis=1, keepdims=True), (N_STATES, V7X_LANES))
        cs_ref[0, c, N_STATES:, :] = jnp.broadcast_to(
            jnp.sum(lf, axis=1, keepdims=True), (N_STATES, V7X_LANES))
        for half in range(MCHUNK // V7X_LANES):
            ls = slice(half * V7X_LANES, (half + 1) * V7X_LANES)
            cols_t = jnp.concatenate([cmr[:, ls], b[:, ls], pad_rows], axis=0).T
            ccol_ref[0, c, ls, :] = cols_t[:, :2 * N_STATES]

    row = lax.broadcasted_iota(jnp.int32, (ts, COL_BLOCK), 0)
    has_prev = tile > 0
    has_next = tile < pl.num_programs(1) - 1
    for c0 in range(0, D_MODEL, COL_BLOCK):
        cols = slice(c0, c0 + COL_BLOCK)
        u = mm(OFF_CC + c0) * mm(OFF_CX + c0)
        edge_prev = jnp.where(has_prev, up_ref[0, 0, 1:2, cols], 0.0)
        edge_next = jnp.where(has_next, un_ref[0, 0, 0:1, cols], 0.0)
        u_prev = jnp.where(row == 0, edge_prev, pltpu.roll(u, 1, 0))
        u_next = jnp.where(row == ts - 1, edge_next, pltpu.roll(u, ts - 1, 0))
        conv = (u_prev * cw_ref[0, 0:1, cols] + u * cw_ref[0, 1:2, cols]
                + u_next * cw_ref[0, 2:3, cols] + cbias_ref[0, 0:1, cols])
        cz = mm(OFF_CZ + c0)
        y = mm(OFF_CB + c0) * conv * (cz * _sigmoid(cz))
        yconv_scr[:, cols] = y.astype(BF16)

    for c0 in range(0, D_MODEL, COL_BLOCK):
        cols = slice(c0, c0 + COL_BLOCK)
        yc = jnp.dot(yconv_scr[...], wpc_ref[0, :, cols], preferred_element_type=F32)
        yc_ref[0, :, cols] = (_sigmoid(mm(OFF_GA + c0)) * yc).astype(BF16)
        if c0 > 0:
            qv_block(c0)


def _layer_block(layer, *shape):
    return pl.BlockSpec((1,) + shape, lambda b, i: (layer,) + (0,) * len(shape),
                        pipeline_mode=pl.Buffered(1))


def _proj(x, u_edges, mod, layer, w_a, w_kg, w_pc, conv_w, conv_b, b_g):
    bsz, seq, _ = x.shape
    ts = _token_tile(seq)
    n_tiles = seq // ts
    n_chunks = seq // MCHUNK
    cps = ts // MCHUNK
    const = functools.partial(_layer_block, layer)
    tok = pl.BlockSpec((1, ts, D_MODEL), lambda b, i: (b, i, 0))
    per_row = pl.BlockSpec((1, 1, D_MODEL), lambda b, i: (mod.base + b, 0, 0))
    edge = lambda index: pl.BlockSpec((1, 1, EDGE_ROWS, D_MODEL),
                                      lambda b, i: (b, index(i), 0, 0))
    chunked = lambda r, c: pl.BlockSpec((1, cps, r, c), lambda b, i: (b, i, 0, 0))
    return pl.pallas_call(
        _proj_kernel,
        grid=(bsz, n_tiles),
        in_specs=[
            tok,
            edge(lambda i: jnp.maximum(i - 1, 0)), edge(lambda i: jnp.minimum(i + 1, n_tiles - 1)),
            per_row, per_row,
            _layer_block(0, D_MODEL, W_A_COLS), const(D_MODEL + N_GATES, D_MODEL),
            const(D_MODEL, D_MODEL), const(3, D_MODEL), const(1, D_MODEL), const(N_GATES, 1),
        ],
        out_specs=[
            chunked(MCHUNK, D_MODEL), chunked(D_MODEL, MCHUNK), chunked(MCHUNK, D_MODEL),
            tok, tok,
            chunked(N_STATES, MCHUNK), chunked(MCHUNK, 2 * N_STATES),
            chunked(2 * N_STATES, V7X_LANES),
        ],
        out_shape=[
            jax.ShapeDtypeStruct((bsz, n_chunks, MCHUNK, D_MODEL), BF16),
            jax.ShapeDtypeStruct((bsz, n_chunks, D_MODEL, MCHUNK), BF16),
            jax.ShapeDtypeStruct((bsz, n_chunks, MCHUNK, D_MODEL), BF16),
            jax.ShapeDtypeStruct((bsz, seq, D_MODEL), BF16),
            jax.ShapeDtypeStruct((bsz, seq, D_MODEL), BF16),
            jax.ShapeDtypeStruct((bsz, n_chunks, N_STATES, MCHUNK), F32),
            jax.ShapeDtypeStruct((bsz, n_chunks, MCHUNK, 2 * N_STATES), F32),
            jax.ShapeDtypeStruct((bsz, n_chunks, 2 * N_STATES, V7X_LANES), F32),
        ],
        scratch_shapes=[pltpu.VMEM((ts, D_MODEL), BF16)],
        compiler_params=pltpu.CompilerParams(
            dimension_semantics=("parallel", "parallel"), vmem_limit_bytes=58 * 2**20),
        name="proj",
    )(x, u_edges, u_edges, mod.a, mod.shift, w_a, w_kg, w_pc, conv_w, conv_b, b_g)


def _mlstm_kernel(qf_ref, ktf_ref, vf_ref, rf_ref, cf_ref, sf_ref,
                  qb_ref, ktb_ref, vb_ref, rb_ref, cb_ref, sb_ref,
                  hf_ref, hb_ref,
                  c_scr, cbf_scr, n_scr, nbf_scr, m_scr):
    cps = qf_ref.shape[1]

    @pl.when(pl.program_id(1) == 0)
    def _():
        c_scr[...] = jnp.zeros_like(c_scr)
        cbf_scr[...] = jnp.zeros_like(cbf_scr)
        n_scr[...] = jnp.zeros_like(n_scr)
        nbf_scr[...] = jnp.zeros_like(nbf_scr)
        m_scr[...] = jnp.zeros_like(m_scr)

    t_idx = lax.broadcasted_iota(jnp.int32, (MCHUNK, MCHUNK), 0)
    s_idx = lax.broadcasted_iota(jnp.int32, (MCHUNK, MCHUNK), 1)
    masks = (s_idx <= t_idx, s_idx >= t_idx)
    ones = jnp.ones((MCHUNK, V7X_LANES), BF16)
    dirs = ((qf_ref, ktf_ref, vf_ref, rf_ref, cf_ref, sf_ref, hf_ref),
            (qb_ref, ktb_ref, vb_ref, rb_ref, cb_ref, sb_ref, hb_ref))

    def chunk_body(step, carry):
        for d, (q_ref, kt_ref, v_ref, r_ref, col_ref, cs_ref, h_ref) in enumerate(dirs):
            c = step if d == 0 else cps - 1 - step
            for hh in range(N_HEADS):
                i = d * N_HEADS + hh
                hs = slice(hh * HEAD_DIM, (hh + 1) * HEAD_DIM)
                q = q_ref[0, c, :, hs]
                kt = kt_ref[0, c, hs, :]
                v = v_ref[0, c, :, hs]
                r_row = r_ref[0, c, i:i + 1, :]
                cmr = jnp.broadcast_to(col_ref[0, c, :, i:i + 1], (MCHUNK, V7X_LANES))
                b_cum = jnp.broadcast_to(col_ref[0, c, :, N_STATES + i:N_STATES + i + 1],
                                         (MCHUNK, V7X_LANES))
                r_max = cs_ref[0, c, i:i + 1, :]
                b_tot = cs_ref[0, c, N_STATES + i:N_STATES + i + 1, :]
                m_row = m_scr[i]

                big_m = jnp.maximum(cmr, m_row)
                d_exp = jnp.exp(jnp.where(masks[d], r_row - _twice(big_m), NEG_BIG))
                p = jnp.dot(q, kt, preferred_element_type=F32) * d_exp
                p_bf = p.astype(BF16)
                inter_w = jnp.exp(m_row - big_m)
                q_n = jnp.dot(q, nbf_scr[i], preferred_element_type=F32)
                p_sum = jnp.dot(p_bf, ones, preferred_element_type=F32)
                den = jnp.maximum(jnp.abs(p_sum + inter_w * q_n), jnp.exp(-(b_cum + big_m)))
                inv = 1.0 / den
                q_c = jnp.dot(q, cbf_scr[i], preferred_element_type=F32)
                num = jnp.dot(p_bf, v, preferred_element_type=F32) + _twice(inter_w) * q_c
                h_ref[0, c, :, hs] = (num * _twice(inv)).astype(BF16)

                m_x = jnp.maximum(m_row, r_max)
                w_row = jnp.exp(r_row - _twice(m_x))
                decay = jnp.exp(m_row - m_x)
                ktw = kt * w_row.astype(BF16)
                c_new = _twice(decay) * c_scr[i] + jnp.dot(ktw, v, preferred_element_type=F32)
                c_scr[i] = c_new
                cbf_scr[i] = c_new.astype(BF16)
                n_new = decay * n_scr[i] + jnp.dot(ktw, ones, preferred_element_type=F32)
                n_scr[i] = n_new
                nbf_scr[i] = n_new.astype(BF16)
                m_scr[i] = b_tot + m_x
        return carry

    lax.fori_loop(0, cps, chunk_body, 0)


def _mlstm(q4, kt4, v4, rrow, ccol, cs):
    bsz, n_chunks = q4.shape[:2]
    cps = min(MLSTM_CHUNKS, n_chunks)
    n_steps = n_chunks // cps

    def specs(index):
        blk = lambda r, c: pl.BlockSpec((1, cps, r, c), lambda b, j: (b, index(j), 0, 0))
        return [blk(MCHUNK, D_MODEL), blk(D_MODEL, MCHUNK), blk(MCHUNK, D_MODEL),
                blk(N_STATES, MCHUNK), blk(MCHUNK, 2 * N_STATES), blk(2 * N_STATES, V7X_LANES)]

    fwd = lambda j: j
    bwd = lambda j: n_steps - 1 - j
    h_shape = jax.ShapeDtypeStruct(q4.shape, BF16)
    return pl.pallas_call(
        _mlstm_kernel,
        grid=(bsz, n_steps),
        in_specs=specs(fwd) + specs(bwd),
        out_specs=[specs(fwd)[0], specs(bwd)[0]],
        out_shape=[h_shape, h_shape],
        scratch_shapes=[
            pltpu.VMEM((N_STATES, HEAD_DIM, HEAD_DIM), F32),
            pltpu.VMEM((N_STATES, HEAD_DIM, HEAD_DIM), BF16),
            pltpu.VMEM((N_STATES, HEAD_DIM, V7X_LANES), F32),
            pltpu.VMEM((N_STATES, HEAD_DIM, V7X_LANES), BF16),
            pltpu.VMEM((N_STATES, 1, V7X_LANES), F32),
        ],
        compiler_params=pltpu.CompilerParams(
            dimension_semantics=("parallel", "arbitrary"), vmem_limit_bytes=52 * 2**20),
        name="mlstm",
    )(q4, kt4, v4, rrow, ccol, cs, q4, kt4, v4, rrow, ccol, cs)


def _out_kernel(hf_ref, hb_ref, yc_ref, h_ref, x_ref, gate_ref, mhg_ref, wb_ref, wpm_ref,
                wout_ref, fg_ref, o_ref, *rest, final):
    ym_scr, merged_scr = rest[-2:]
    ts = x_ref.shape[1]

    def mm(col, width):
        return jnp.dot(h_ref[0], wb_ref[0, :, col:col + width], preferred_element_type=F32)

    for hh in range(N_HEADS):
        hs = slice(hh * HEAD_DIM, (hh + 1) * HEAD_DIM)
        mz = mm(OFF_MZ + hh * HEAD_DIM, HEAD_DIM)
        og = _sigmoid(mm(OFF_O + hh * HEAD_DIM, HEAD_DIM)) * (mz * _sigmoid(mz))
        hm = hf_ref[0, :, hs].astype(F32) + hb_ref[0, :, hs].astype(F32)
        hm = hm * _rms_scale(hm) * mhg_ref[0, 0:1, hs]
        ym_scr[:, hs] = (og * hm).astype(BF16)
    for c0 in range(0, D_MODEL, COL_BLOCK):
        cols = slice(c0, c0 + COL_BLOCK)
        ym = jnp.dot(ym_scr[...], wpm_ref[0, :, cols], preferred_element_type=F32)
        merged = yc_ref[0, :, cols].astype(F32) + _sigmoid(mm(OFF_GB + c0, COL_BLOCK)) * ym
        merged_scr[:, cols] = merged.astype(BF16)
    for c0 in range(0, D_MODEL, COL_BLOCK):
        cols = slice(c0, c0 + COL_BLOCK)
        y = jnp.dot(merged_scr[...], wout_ref[0, :, cols], preferred_element_type=F32)
        o_ref[0, :, cols] = x_ref[0, :, cols] + gate_ref[0, :, cols] * y
    if final:
        x_new = o_ref[0]
        o_ref[0] = x_new * _rms_scale(x_new) * fg_ref[...]
    else:
        e_ref = rest[0]
        e_ref[0, 0, 0:1, :] = o_ref[0, 0:1, :]
        e_ref[0, 0, 1:2, :] = o_ref[0, ts - 1:ts, :]
        e_ref[0, 0, 2:EDGE_ROWS, :] = jnp.zeros((EDGE_ROWS - 2, D_MODEL), F32)


def _out(hf, hb, yc, h, x, mod, layer, mh_g, w_b, w_pm, w_out, final_g, final):
    bsz, seq, _ = x.shape
    ts = _token_tile(seq)
    n_tiles = seq // ts
    const = functools.partial(_layer_block, layer)
    tok = pl.BlockSpec((1, ts, D_MODEL), lambda b, i: (b, i, 0))
    per_row = pl.BlockSpec((1, 1, D_MODEL), lambda b, i: (mod.base + b, 0, 0))
    out_specs = [tok]
    out_shape = [jax.ShapeDtypeStruct(x.shape, F32)]
    if not final:
        out_specs.append(pl.BlockSpec((1, 1, EDGE_ROWS, D_MODEL), lambda b, i: (b, i, 0, 0)))
        out_shape.append(jax.ShapeDtypeStruct((bsz, n_tiles, EDGE_ROWS, D_MODEL), F32))
    tile_bf16 = pltpu.VMEM((ts, D_MODEL), BF16)
    return pl.pallas_call(
        functools.partial(_out_kernel, final=final),
        grid=(bsz, seq // ts),
        in_specs=[tok, tok, tok, tok, tok, per_row,
                  const(1, D_MODEL), _layer_block(0, D_MODEL, W_B_COLS), const(D_MODEL, D_MODEL),
                  const(D_MODEL, D_MODEL),
                  pl.BlockSpec((1, D_MODEL), lambda b, i: (0, 0))],
        out_specs=out_specs,
        out_shape=out_shape,
        scratch_shapes=[tile_bf16, tile_bf16],
        compiler_params=pltpu.CompilerParams(
            dimension_semantics=("parallel", "parallel"), vmem_limit_bytes=56 * 2**20),
        name="out",
    )(hf, hb, yc, h, x, mod.gate, mh_g, w_b, w_pm, w_out, final_g)


def kernel(x_prompt, x_sample, c_prompt, c_sample, w_ada, b_ada, norm_g, w_in, b_gates, conv_w,
           conv_b, mh_norm_g, w_proj_conv, w_proj_mlstm, w_out, final_norm_g):
    depth = w_ada.shape[0]
    xs = [x_prompt, x_sample]
    n_rows = sum(x.shape[0] for x in xs)
    mod_rows = _ada(jnp.concatenate([c_prompt, c_sample], axis=0), w_ada, b_ada, norm_g)
    a_rows, shift_rows, gate_rows = (m.reshape(depth * n_rows, 1, D_MODEL) for m in mod_rows)

    gm0 = 9 * D_MODEL + N_GATES
    w_kg = _wkg(w_in)
    w_pc, w_pm, w_o = (w.astype(BF16) for w in (w_proj_conv, w_proj_mlstm, w_out))
    b_g = b_gates.reshape(depth, N_GATES)[:, jnp.array(GATE_PERM)].reshape(depth, N_GATES, 1)
    cbias = conv_b.reshape(depth, 1, D_MODEL)
    mh_g = mh_norm_g.reshape(depth, 1, D_MODEL)
    final_g = final_norm_g.reshape(1, D_MODEL)
    edges = [_edges(x) for x in xs]

    for l in range(depth):
        w = w_in[l]
        col = lambda i: w[:, i * D_MODEL:(i + 1) * D_MODEL]
        w_a = jnp.concatenate([col(0), col(1), col(2), col(3), col(4), col(6),
                               w[:, gm0:gm0 + D_MODEL]], axis=1).astype(BF16)[None]
        w_b = jnp.concatenate([col(7), col(8), w[:, gm0 + D_MODEL:]], axis=1).astype(BF16)[None]
        final = l == depth - 1
        new_xs, new_edges, row0 = [], [], 0
        for x, edge in zip(xs, edges):
            mod = _Modulation(a_rows, shift_rows, gate_rows, l * n_rows + row0)
            row0 += x.shape[0]
            u_edges = _halo(edge, mod, w_a, l)
            q4, kt4, v4, yc, h, rrow, ccol, cs = _proj(
                x, u_edges, mod, l, w_a, w_kg, w_pc, conv_w, cbias, b_g)
            hf, hb = _mlstm(q4, kt4, v4, rrow, ccol, cs)
            res = _out(hf.reshape(x.shape), hb.reshape(x.shape), yc, h, x, mod, l,
                       mh_g, w_b, w_pm, w_o, final_g, final=final)
            new_xs.append(res[0])
            new_edges.append(None if final else res[1])
        xs, edges = new_xs, new_edges
    return tuple(xs)
```
